```python
import math
import jax, jax.numpy as jnp
from jax import lax
import numpy as np

D_MODEL = 1024
BATCH = 8
SEQ = 4096
DEPTH = 2

N_A_LAYERS = DEPTH // 2
N_B_LAYERS = DEPTH - N_A_LAYERS
SSM_EXPAND = 2
D_INNER = SSM_EXPAND * D_MODEL
SSM_HEAD_DIM = 64
SSM_HEADS = D_INNER // SSM_HEAD_DIM
SSM_GROUPS = 4
SSM_STATE = 128
SSM_CONV = 4
SSM_CHUNK = 128
GN = SSM_GROUPS * SSM_STATE
CONV_DIM = D_INNER + 2 * GN
IN_PROJ_DIM = D_INNER + CONV_DIM + SSM_HEADS
SB_HEADS = 16
SB_HEAD_DIM = D_MODEL // SB_HEADS
SB_BLOCK = 128
D_FF = 2816
FFN_CONV = 3
EPS = 1e-6

kernel_name = 'yoco_mamba2_stickbreaking_convffn'


def rmsnorm(x, w):
    xf = x.astype(jnp.float32)
    y = xf * lax.rsqrt(jnp.mean(xf * xf, axis=-1, keepdims=True) + EPS)
    return (y * w.astype(jnp.float32)).astype(x.dtype)


def causal_dwconv(x, w, b):
    k_w, s = w.shape[0], x.shape[1]
    xp = jnp.pad(x, ((0, 0), (k_w - 1, 0), (0, 0)))
    out = b
    for j in range(k_w):
        out = out + xp[:, j:j + s, :] * w[j]
    return out


def segsum(a):
    t = a.shape[-1]
    cs = jnp.cumsum(a, axis=-1)
    diff = cs[..., :, None] - cs[..., None, :]
    mask = jnp.tril(jnp.ones((t, t), dtype=bool))
    return jnp.where(mask, diff, -jnp.inf)


def ssd_chunked(xs, dt, a, bm, cm):
    bsz, s, h, p = xs.shape
    g, n = bm.shape[2], bm.shape[3]
    r = h // g
    c, l = s // SSM_CHUNK, SSM_CHUNK
    xd = (xs * dt[..., None]).reshape(bsz, c, l, g, r, p)
    ad = (dt * a).reshape(bsz, c, l, g, r).transpose(0, 3, 4, 1, 2)
    bc = bm.reshape(bsz, c, l, g, n)
    cc = cm.reshape(bsz, c, l, g, n)
    a_cs = jnp.cumsum(ad, axis=-1)
    lmat = jnp.exp(segsum(ad))
    cb = jnp.einsum('bclgn,bcsgn->bcgls', cc, bc)
    y_diag = jnp.einsum('bcgls,bgrcls,bcsgrp->bclgrp', cb, lmat, xd)
    decay_states = jnp.exp(a_cs[..., -1:] - a_cs)
    states = jnp.einsum('bclgn,bgrcl,bclgrp->bcgrpn', bc, decay_states, xd)
    chunk_decay = jnp.exp(a_cs[..., -1])

    def step(carry, inp):
        st, dc = inp
        return carry * dc[..., None, None] + st, carry

    init = jnp.zeros((bsz, g, r, p, n), dtype=states.dtype)
    _, prev = lax.scan(step, init, (jnp.moveaxis(states, 1, 0), jnp.moveaxis(chunk_decay, -1, 0)))
    prev = jnp.moveaxis(prev, 0, 1)
    y_off = jnp.einsum('bclgn,bcgrpn,bgrcl->bclgrp', cc, prev, jnp.exp(a_cs))
    return (y_diag + y_off).reshape(bsz, s, h, p)


def mamba2_mixer(u, w_in, conv_w, conv_b, dt_bias, a_log, d_skip, gate_norm_w, w_out):
    bsz, s, _ = u.shape
    f32 = jnp.float32
    zxbcdt = u @ w_in
    z = zxbcdt[..., :D_INNER]
    xbc = zxbcdt[..., D_INNER:D_INNER + CONV_DIM]
    dt = zxbcdt[..., D_INNER + CONV_DIM:]
    xbc = jax.nn.silu(causal_dwconv(xbc, conv_w, conv_b))
    xs = xbc[..., :D_INNER].reshape(bsz, s, SSM_HEADS, SSM_HEAD_DIM).astype(f32)
    bm = xbc[..., D_INNER:D_INNER + GN].reshape(bsz, s, SSM_GROUPS, SSM_STATE).astype(f32)
    cm = xbc[..., D_INNER + GN:].reshape(bsz, s, SSM_GROUPS, SSM_STATE).astype(f32)
    dt = jax.nn.softplus(dt.astype(f32) + dt_bias.astype(f32))
    a = -jnp.exp(a_log.astype(f32))
    y = ssd_chunked(xs, dt, a, bm, cm)
    y = y + d_skip.astype(f32)[:, None] * xs
    y = y.reshape(bsz, s, D_INNER) * jax.nn.silu(z.astype(f32))
    gsz = D_INNER // SSM_GROUPS
    y = rmsnorm(y.reshape(bsz, s, SSM_GROUPS, gsz), gate_norm_w.reshape(SSM_GROUPS, gsz))
    return y.reshape(bsz, s, D_INNER).astype(u.dtype) @ w_out


def stick_breaking_attention(q, k, v):
    bsz, s, h, d = q.shape
    nblk = s // SB_BLOCK
    scale = 1.0 / math.sqrt(d)
    qb = q.reshape(bsz, nblk, SB_BLOCK, h, d).transpose(1, 0, 3, 2, 4)
    kt = k.transpose(0, 2, 1, 3).astype(jnp.float32)
    vt = v.transpose(0, 2, 1, 3).astype(jnp.float32)
    starts = jnp.arange(nblk, dtype=jnp.int32) * SB_BLOCK
    kpos = jnp.arange(s, dtype=jnp.int32)

    def block(args):
        qblk, i0 = args
        qpos = i0 + jnp.arange(SB_BLOCK, dtype=jnp.int32)
        mask = kpos[None, :] < qpos[:, None]
        zl = jnp.einsum('bhqd,bhkd->bhqk', qblk.astype(jnp.float32), kt) * scale
        log_1m = jnp.where(mask, -jax.nn.softplus(zl), 0.0)
        later = lax.cumsum(log_1m, axis=3, reverse=True) - log_1m
        att = jnp.where(mask, jnp.exp(jax.nn.log_sigmoid(zl) + later), 0.0)
        return jnp.einsum('bhqk,bhkd->bhqd', att, vt)

    o = lax.map(block, (qb, starts))
    return o.transpose(1, 0, 3, 2, 4).reshape(bsz, s, h * d).astype(q.dtype)


def conv_ffn(u, w_up, conv_w, conv_b, w_down):
    hid = causal_dwconv(u @ w_up, conv_w, conv_b)
    gate, val = hid[..., :D_FF], hid[..., D_FF:]
    return (jax.nn.silu(gate) * val) @ w_down


def setup_inputs(seed: int = 0) -> dict:
    key = jax.random.key(seed)
    ks = jax.random.split(key, 24)
    f32 = jnp.float32

    def nrm(k, shape, scale):
        return jax.random.normal(k, shape, f32) * scale

    def gain(k, shape):
        return 1.0 + 0.02 * jax.random.normal(k, shape, f32)

    na, nb = N_A_LAYERS, N_B_LAYERS
    dt0 = jnp.exp(jax.random.uniform(ks[5], (na, SSM_HEADS), f32, math.log(1e-3), math.log(1e-1)))
    dt_bias = dt0 + jnp.log(-jnp.expm1(-dt0))
    a_log = jnp.log(jax.random.uniform(ks[6], (na, SSM_HEADS), f32, 1.0, 16.0))
    return {
        'x': jax.random.normal(ks[0], (BATCH, SEQ, D_MODEL), f32),
        'ssm_norm_w': gain(ks[1], (na, D_MODEL)),
        'ssm_in_w': nrm(ks[2], (na, D_MODEL, IN_PROJ_DIM), D_MODEL ** -0.5),
        'ssm_conv_w': nrm(ks[3], (na, SSM_CONV, CONV_DIM), SSM_CONV ** -0.5),
        'ssm_conv_b': nrm(ks[4], (na, CONV_DIM), 0.02),
        'ssm_dt_bias': dt_bias,
        'ssm_a_log': a_log,
        'ssm_d': 1.0 + 0.1 * jax.random.normal(ks[7], (na, SSM_HEADS), f32),
        'ssm_gate_norm_w': gain(ks[8], (na, D_INNER)),
        'ssm_out_w': nrm(ks[9], (na, D_INNER, D_MODEL), D_INNER ** -0.5),
        'kv_norm_w': gain(ks[10], (D_MODEL,)),
        'w_k': nrm(ks[11], (D_MODEL, SB_HEADS * SB_HEAD_DIM), D_MODEL ** -0.5),
        'w_v': nrm(ks[12], (D_MODEL, SB_HEADS * SB_HEAD_DIM), D_MODEL ** -0.5),
        'attn_norm_w': gain(ks[13], (nb, D_MODEL)),
        'w_q': nrm(ks[14], (nb, D_MODEL, SB_HEADS * SB_HEAD_DIM), D_MODEL ** -0.5),
        'w_o': nrm(ks[15], (nb, SB_HEADS * SB_HEAD_DIM, D_MODEL), D_MODEL ** -0.5),
        'ffn_norm_w': gain(ks[16], (DEPTH, D_MODEL)),
        'ffn_up_w': nrm(ks[17], (DEPTH, D_MODEL, 2 * D_FF), D_MODEL ** -0.5),
        'ffn_conv_w': nrm(ks[18], (DEPTH, FFN_CONV, 2 * D_FF), FFN_CONV ** -0.5),
        'ffn_conv_b': nrm(ks[19], (DEPTH, 2 * D_FF), 0.02),
        'ffn_down_w': nrm(ks[20], (DEPTH, D_FF, D_MODEL), D_FF ** -0.5),
        'final_norm_w': gain(ks[21], (D_MODEL,)),
    }


def reference(x, ssm_norm_w, ssm_in_w, ssm_conv_w, ssm_conv_b, ssm_dt_bias, ssm_a_log, ssm_d,
              ssm_gate_norm_w, ssm_out_w, kv_norm_w, w_k, w_v, attn_norm_w, w_q, w_o,
              ffn_norm_w, ffn_up_w, ffn_conv_w, ffn_conv_b, ffn_down_w, final_norm_w):
    bsz, s, _ = x.shape
    h = x
    k_shared = None
    v_shared = None
    for layer in range(DEPTH):
        if layer < N_A_LAYERS:
            i = layer
            h = h + mamba2_mixer(rmsnorm(h, ssm_norm_w[i]), ssm_in_w[i], ssm_conv_w[i], ssm_conv_b[i],
                                 ssm_dt_bias[i], ssm_a_log[i], ssm_d[i], ssm_gate_norm_w[i], ssm_out_w[i])
        else:
            i = layer - N_A_LAYERS
            if k_shared is None:
                hk = rmsnorm(h, kv_norm_w)
                k_shared = (hk @ w_k).reshape(bsz, s, SB_HEADS, SB_HEAD_DIM)
                v_shared = (hk @ w_v).reshape(bsz, s, SB_HEADS, SB_HEAD_DIM)
            q = (rmsnorm(h, attn_norm_w[i]) @ w_q[i]).reshape(bsz, s, SB_HEADS, SB_HEAD_DIM)
            h = h + stick_breaking_attention(q, k_shared, v_shared) @ w_o[i]
        h = h + conv_ffn(rmsnorm(h, ffn_norm_w[layer]), ffn_up_w[layer], ffn_conv_w[layer],
                         ffn_conv_b[layer], ffn_down_w[layer])
    return rmsnorm(h, final_norm_w)
```

```python
import functools
import math

import jax
import jax.numpy as jnp
from jax import lax
from jax.experimental import pallas as pl
from jax.experimental.pallas import tpu as pltpu

D_MODEL = 1024
D_INNER = 2048
SSM_HEADS = 32
SSM_HEAD_DIM = 64
SSM_GROUPS = 4
SSM_STATE = 128
SSM_CONV = 4
SSM_CHUNK = 128
GN = SSM_GROUPS * SSM_STATE
CONV_DIM = D_INNER + 2 * GN
SB_HEADS = 16
SB_HEAD_DIM = 64
SB_BLOCK = 128
D_FF = 2816
FFN_CONV = 3
EPS = 1e-6

LANES = 128
HALO = 16
ROW_TILE = 512
COL_CHUNK = 512
FFN_CHUNK = 256
VMEM_LIMIT = 56 * 1024 * 1024
NEG_BIG = -1e30

F32 = jnp.float32
BF16 = jnp.bfloat16


def _rms(x, w):
    ms = jnp.mean(x * x, axis=-1, keepdims=True)
    return x * lax.rsqrt(ms + EPS) * w


def _softplus(x):
    return jnp.maximum(x, 0.0) + jnp.log(1.0 + jnp.exp(-jnp.abs(x)))


def _silu(x):
    return x / (1.0 + jnp.exp(-x))


def _dot(a, b):
    return jnp.dot(a, b, preferred_element_type=F32)


def _dot_nt(a, b):
    return lax.dot_general(a, b, (((1,), (1,)), ((), ())), preferred_element_type=F32)


def _split3(x):
    h1 = x.astype(BF16)
    r1 = x - h1.astype(F32)
    h2 = r1.astype(BF16)
    h3 = (r1 - h2.astype(F32)).astype(BF16)
    return h1, h2, h3


def _resident(shape):
    return pl.BlockSpec(shape, lambda *_: (0,) * len(shape), pipeline_mode=pl.Buffered(1))


def _fill_normed(xn_ref, x_ref, xh_ref, nw, first):
    xn_ref[HALO:, :] = _rms(x_ref[...], nw).astype(BF16)
    xh = _rms(xh_ref[...], nw)
    xn_ref[:HALO, :] = jnp.where(first, 0.0, xh).astype(BF16)


def _causal_conv(h_ref, cw_ref, cb_ref, lo, width, taps, rows):
    y = cb_ref[:, lo:lo + width]
    for k in range(taps):
        y = y + h_ref[pl.ds(HALO - (taps - 1) + k, rows), :] * cw_ref[k:k + 1, lo:lo + width]
    return y


def _in_proj_kernel(x_ref, xh_ref, nw_ref, w_ref, wdt_ref, cw_ref, cb_ref, dtb_ref,
                    out_ref, dt_ref, xn_ref, acc_ref, *, tm, tiles_per_seq):
    first = (pl.program_id(0) % tiles_per_seq) == 0
    _fill_normed(xn_ref, x_ref, xh_ref, nw_ref[...], first)
    xn = xn_ref[HALO:, :]
    dt_ref[...] = _softplus(_dot(xn, wdt_ref[...]) + dtb_ref[...])
    for c in range(D_INNER // COL_CHUNK):
        lo = c * COL_CHUNK
        out_ref[:, lo:lo + COL_CHUNK] = _dot(xn, w_ref[:, lo:lo + COL_CHUNK]).astype(BF16)
    for c in range(CONV_DIM // COL_CHUNK):
        lo = c * COL_CHUNK
        acc_ref[...] = _dot(xn_ref[...], w_ref[:, D_INNER + lo:D_INNER + lo + COL_CHUNK])
        y = _causal_conv(acc_ref, cw_ref, cb_ref, lo, COL_CHUNK, SSM_CONV, tm)
        out_ref[:, D_INNER + lo:D_INNER + lo + COL_CHUNK] = _silu(y).astype(BF16)


def _row_specs(tm, d):
    per = tm // HALO
    return (pl.BlockSpec((tm, d), lambda i: (i, 0)),
            pl.BlockSpec((HALO, d), lambda i: (jnp.maximum(i * per - 1, 0), 0)))


def _in_proj(h, seq, norm_w, w_in, conv_w, conv_b, dt_bias):
    t, d = h.shape
    tm = min(ROW_TILE, seq)
    w_main = w_in[:, :D_INNER + CONV_DIM].astype(BF16)
    w_dt = jnp.pad(w_in[:, D_INNER + CONV_DIM:], ((0, 0), (0, LANES - SSM_HEADS))).astype(BF16)
    dtb = jnp.pad(dt_bias, (0, LANES - SSM_HEADS)).reshape(1, LANES)
    x_spec, xh_spec = _row_specs(tm, d)
    return pl.pallas_call(
        functools.partial(_in_proj_kernel, tm=tm, tiles_per_seq=seq // tm),
        grid=(t // tm,),
        in_specs=[x_spec, xh_spec, _resident((1, d)), _resident(w_main.shape), _resident(w_dt.shape),
                  _resident(conv_w.shape), _resident((1, CONV_DIM)), _resident((1, LANES))],
        out_specs=[pl.BlockSpec((tm, D_INNER + CONV_DIM), lambda i: (i, 0)),
                   pl.BlockSpec((tm, LANES), lambda i: (i, 0))],
        out_shape=[jax.ShapeDtypeStruct((t, D_INNER + CONV_DIM), BF16),
                   jax.ShapeDtypeStruct((t, LANES), F32)],
        scratch_shapes=[pltpu.VMEM((tm + HALO, d), BF16), pltpu.VMEM((tm + HALO, COL_CHUNK), F32)],
        compiler_params=pltpu.CompilerParams(dimension_semantics=("arbitrary",), vmem_limit_bytes=VMEM_LIMIT),
        name="ssm_in_proj",
    )(h, h, norm_w.reshape(1, d), w_main, w_dt, conv_w, conv_b.reshape(1, CONV_DIM), dtb)


def _pair_cols(arr, h0, low_half):
    return jnp.where(low_half, arr[:, h0:h0 + 1], arr[:, h0 + 1:h0 + 2])


def _ssd_kernel(z_ref, xs_ref, b_ref, c_ref, dt_ref, alog_ref, dskip_ref, gnw_ref,
                y_ref, state_ref, yg_ref):
    @pl.when(pl.program_id(1) == 0)
    def _():
        state_ref[...] = jnp.zeros_like(state_ref)

    n = SSM_CHUNK
    row = lax.broadcasted_iota(jnp.int32, (n, n), 0)
    col = lax.broadcasted_iota(jnp.int32, (n, n), 1)
    causal = col <= row
    low_half = col < SSM_HEAD_DIM
    low_half_row = low_half[:1, :]
    low_half_bf = lax.broadcasted_iota(jnp.int32, (n, n), 1) < SSM_HEAD_DIM

    dt = dt_ref[...]
    ad = dt * (-jnp.exp(alog_ref[...]))
    tri = causal.astype(BF16)
    a1, a2, a3 = _split3(ad)
    acs = _dot(tri, a1) + _dot(tri, a2) + _dot(tri, a3)
    acs_last = acs[n - 1:n, :]
    acs_t = acs.T
    dt_t = dt.T
    e_acs = jnp.exp(acs)
    w_dec = jnp.exp(acs_last - acs) * dt
    c_dec = jnp.exp(acs_last)

    for g in range(SSM_GROUPS):
        gl = g * SSM_STATE
        bg = b_ref[:, gl:gl + SSM_STATE]
        cg = c_ref[:, gl:gl + SSM_STATE]
        cb = _dot_nt(cg, bg)
        bg_t = bg.astype(F32).T.astype(BF16)
        for p in range(SSM_HEADS // SSM_GROUPS // 2):
            h0 = g * (SSM_HEADS // SSM_GROUPS) + 2 * p
            lo = h0 * SSM_HEAD_DIM
            xs_pair = xs_ref[:, lo:lo + LANES]
            xs_f = xs_pair.astype(F32)
            ms = []
            for h in (h0, h0 + 1):
                diff = acs[:, h:h + 1] - acs_t[h:h + 1, :]
                lmat = jnp.exp(jnp.where(causal, diff, NEG_BIG))
                ms.append((cb * lmat * dt_t[h:h + 1, :]).astype(BF16))
            zero = jnp.zeros_like(xs_pair)
            rhs = jnp.concatenate([jnp.where(low_half_bf, xs_pair, zero),
                                   jnp.where(low_half_bf, zero, xs_pair)], axis=0)
            y_diag = _dot(jnp.concatenate(ms, axis=1), rhs)
            s_pair = state_ref[:, lo:lo + LANES]
            y_off = _dot(cg, s_pair.astype(BF16)) * _pair_cols(e_acs, h0, low_half)
            xw = (xs_f * _pair_cols(w_dec, h0, low_half)).astype(BF16)
            state_ref[:, lo:lo + LANES] = s_pair * _pair_cols(c_dec, h0, low_half_row) + _dot(bg_t, xw)
            y = y_diag + y_off + dskip_ref[:, lo:lo + LANES] * xs_f
            yg_ref[:, lo:lo + LANES] = y * _silu(z_ref[:, lo:lo + LANES].astype(F32))
        gw = D_INNER // SSM_GROUPS
        yg = yg_ref[:, g * gw:(g + 1) * gw]
        y_ref[:, g * gw:(g + 1) * gw] = _rms(yg, gnw_ref[:, g * gw:(g + 1) * gw]).astype(BF16)


def _ssd(zx, dt, bsz, seq, a_log, d_skip, gate_norm_w):
    t = zx.shape[0]
    nc = seq // SSM_CHUNK
    alog = jnp.pad(a_log, (0, LANES - SSM_HEADS)).reshape(1, LANES)
    dskip = jnp.repeat(d_skip, SSM_HEAD_DIM).reshape(1, D_INNER)
    rowblk = lambda b, c: b * nc + c
    x_blk = D_INNER // D_INNER
    return pl.pallas_call(
        _ssd_kernel,
        grid=(bsz, nc),
        in_specs=[pl.BlockSpec((SSM_CHUNK, D_INNER), lambda b, c: (rowblk(b, c), 0)),
                  pl.BlockSpec((SSM_CHUNK, D_INNER), lambda b, c: (rowblk(b, c), x_blk)),
                  pl.BlockSpec((SSM_CHUNK, GN), lambda b, c: (rowblk(b, c), 2 * D_INNER // GN)),
                  pl.BlockSpec((SSM_CHUNK, GN), lambda b, c: (rowblk(b, c), 2 * D_INNER // GN + 1)),
                  pl.BlockSpec((SSM_CHUNK, LANES), lambda b, c: (rowblk(b, c), 0)),
                  _resident((1, LANES)), _resident((1, D_INNER)), _resident((1, D_INNER))],
        out_specs=pl.BlockSpec((SSM_CHUNK, D_INNER), lambda b, c: (rowblk(b, c), 0)),
        out_shape=jax.ShapeDtypeStruct((t, D_INNER), BF16),
        scratch_shapes=[pltpu.VMEM((SSM_STATE, D_INNER), F32), pltpu.VMEM((SSM_CHUNK, D_INNER), F32)],
        compiler_params=pltpu.CompilerParams(dimension_semantics=("arbitrary", "arbitrary"),
                                             vmem_limit_bytes=VMEM_LIMIT),
        name="ssd_scan",
    )(zx, zx, zx, zx, dt, alog, dskip, gate_norm_w.reshape(1, D_INNER))


def _proj_residual_kernel(a_ref, w_ref, res_ref, out_ref):
    for c in range(D_MODEL // COL_CHUNK):
        lo = c * COL_CHUNK
        out_ref[:, lo:lo + COL_CHUNK] = res_ref[:, lo:lo + COL_CHUNK] + _dot(a_ref[...], w_ref[:, lo:lo + COL_CHUNK])


def _proj_residual(a, w, res, seq):
    t, k = a.shape
    tm = min(ROW_TILE, seq)
    return pl.pallas_call(
        _proj_residual_kernel,
        grid=(t // tm,),
        in_specs=[pl.BlockSpec((tm, k), lambda i: (i, 0)), _resident((k, D_MODEL)),
                  pl.BlockSpec((tm, D_MODEL), lambda i: (i, 0))],
        out_specs=pl.BlockSpec((tm, D_MODEL), lambda i: (i, 0)),
        out_shape=jax.ShapeDtypeStruct((t, D_MODEL), F32),
        compiler_params=pltpu.CompilerParams(dimension_semantics=("arbitrary",), vmem_limit_bytes=VMEM_LIMIT),
        name="proj_residual",
    )(a, w.astype(BF16), res)


def _ffn_kernel(x_ref, xh_ref, nw_ref, wup_ref, cw_ref, cb_ref, wd_ref, fnw_ref,
                out_ref, xn_ref, h_ref, acc_ref, *, tm, tiles_per_seq, final_norm):
    first = (pl.program_id(0) % tiles_per_seq) == 0
    _fill_normed(xn_ref, x_ref, xh_ref, nw_ref[...], first)
    fc = FFN_CHUNK
    for c in range(D_FF // fc):
        lo = 2 * fc * c
        h_ref[...] = _dot(xn_ref[...], wup_ref[:, lo:lo + 2 * fc])
        gv = _causal_conv(h_ref, cw_ref, cb_ref, lo, 2 * fc, FFN_CONV, tm)
        hid = (_silu(gv[:, :fc]) * gv[:, fc:]).astype(BF16)
        contrib = _dot(hid, wd_ref[c * fc:(c + 1) * fc, :])
        if c == 0:
            acc_ref[...] = contrib
        else:
            acc_ref[...] += contrib
    out = x_ref[...] + acc_ref[...]
    if final_norm:
        out = _rms(out, fnw_ref[...])
    out_ref[...] = out


def _interleave_gate_value(a):
    lead = a.shape[:-1]
    g = a[..., :D_FF].reshape(*lead, D_FF // FFN_CHUNK, FFN_CHUNK)
    v = a[..., D_FF:].reshape(*lead, D_FF // FFN_CHUNK, FFN_CHUNK)
    return jnp.stack([g, v], axis=-2).reshape(*lead, 2 * D_FF)


def _ffn(h, seq, norm_w, w_up, conv_w, conv_b, w_down, final_norm_w=None):
    t, d = h.shape
    tm = min(ROW_TILE, seq)
    final_norm = final_norm_w is not None
    fnw = (final_norm_w if final_norm else jnp.ones((d,), F32)).reshape(1, d)
    x_spec, xh_spec = _row_specs(tm, d)
    return pl.pallas_call(
        functools.partial(_ffn_kernel, tm=tm, tiles_per_seq=seq // tm, final_norm=final_norm),
        grid=(t // tm,),
        in_specs=[x_spec, xh_spec, _resident((1, d)), _resident((d, 2 * D_FF)), _resident((FFN_CONV, 2 * D_FF)),
                  _resident((1, 2 * D_FF)), _resident((D_FF, d)), _resident((1, d))],
        out_specs=pl.BlockSpec((tm, d), lambda i: (i, 0)),
        out_shape=jax.ShapeDtypeStruct((t, d), F32),
        scratch_shapes=[pltpu.VMEM((tm + HALO, d), BF16), pltpu.VMEM((tm + HALO, 2 * FFN_CHUNK), F32),
                        pltpu.VMEM((tm, d), F32)],
        compiler_params=pltpu.CompilerParams(dimension_semantics=("arbitrary",), vmem_limit_bytes=VMEM_LIMIT),
        name="conv_ffn",
    )(h, h, norm_w.reshape(1, d), _interleave_gate_value(w_up).astype(BF16), _interleave_gate_value(conv_w),
      _interleave_gate_value(conv_b).reshape(1, 2 * D_FF), w_down.astype(BF16), fnw)


def _qkv_kernel(x_ref, qnw_ref, kvnw_ref, w_ref, out_ref):
    x = x_ref[...]
    xhat = x * lax.rsqrt(jnp.mean(x * x, axis=-1, keepdims=True) + EPS)
    xq = (xhat * qnw_ref[...]).astype(BF16)
    xkv = (xhat * kvnw_ref[...]).astype(BF16)
    for c in range(3 * D_MODEL // COL_CHUNK):
        lo = c * COL_CHUNK
        src = xq if lo < D_MODEL else xkv
        out_ref[:, lo:lo + COL_CHUNK] = _dot(src, w_ref[:, lo:lo + COL_CHUNK]).astype(BF16)


def _qkv_proj(h, seq, q_norm_w, kv_norm_w, w_q, w_k, w_v):
    t, d = h.shape
    tm = min(ROW_TILE, seq)
    w = jnp.concatenate([w_q, w_k, w_v], axis=1).astype(BF16)
    return pl.pallas_call(
        _qkv_kernel,
        grid=(t // tm,),
        in_specs=[pl.BlockSpec((tm, d), lambda i: (i, 0)), _resident((1, d)), _resident((1, d)),
                  _resident((d, 3 * d))],
        out_specs=pl.BlockSpec((tm, 3 * d), lambda i: (i, 0)),
        out_shape=jax.ShapeDtypeStruct((t, 3 * d), BF16),
        compiler_params=pltpu.CompilerParams(dimension_semantics=("arbitrary",), vmem_limit_bytes=VMEM_LIMIT),
        name="qkv_proj",
    )(h, q_norm_w.reshape(1, d), kv_norm_w.reshape(1, d), w)


def _attn_kernel(q_ref, k_ref, v_ref, tt_ref, out_ref, acc_ref, carry_ref):
    i = pl.program_id(2)
    n = SB_BLOCK
    row = lax.broadcasted_iota(jnp.int32, (n, n), 0)
    col = lax.broadcasted_iota(jnp.int32, (n, n), 1)
    low_half = col < SB_HEAD_DIM
    scale = 1.0 / math.sqrt(SB_HEAD_DIM)
    q = q_ref[...]
    zero = jnp.zeros_like(q)
    tt = tt_ref[...]
    outs = []
    for head in range(2):
        qm = jnp.where(low_half if head == 0 else jnp.logical_not(low_half), q, zero)
        acc_ref[...] = jnp.zeros_like(acc_ref)
        carry_ref[...] = jnp.zeros_like(carry_ref)

        def body(t, _):
            j = i - t
            kj = k_ref[pl.ds(pl.multiple_of(j * n, n), n), :]
            vj = v_ref[pl.ds(pl.multiple_of(j * n, n), n), :]
            z = _dot_nt(qm, kj) * scale
            sp = _softplus(z)
            mask = (j * n + col) < (i * n + row)
            spm = jnp.where(mask, sp, 0.0)
            hi = spm.astype(BF16)
            lo = (spm - hi.astype(F32)).astype(BF16)
            cs = _dot(jnp.concatenate([hi, lo], axis=1), tt)
            later = cs[:, :n] + carry_ref[...]
            att = jnp.where(mask, jnp.exp(z - sp - later), 0.0)
            acc_ref[...] += _dot(att.astype(BF16), vj)
            carry_ref[...] += cs[:, n:]
            return 0

        lax.fori_loop(0, i + 1, body, 0)
        outs.append(acc_ref[...])
    out_ref[...] = jnp.where(low_half, outs[0], outs[1]).astype(BF16)


def _attention(qkv, bsz, seq):
    t = qkv.shape[0]
    n = SB_BLOCK
    nq = seq // n
    pairs = SB_HEADS // 2
    r = jnp.arange(n)
    tri = (r[:, None] > r[None, :]).astype(BF16)
    blk = jnp.concatenate([tri, jnp.ones((n, n), BF16)], axis=1)
    tt = jnp.concatenate([blk, blk], axis=0)
    return pl.pallas_call(
        _attn_kernel,
        grid=(bsz, pairs, nq),
        in_specs=[pl.BlockSpec((n, LANES), lambda b, p, i: (b * nq + i, p)),
                  pl.BlockSpec((seq, LANES), lambda b, p, i: (b, pairs + p)),
                  pl.BlockSpec((seq, LANES), lambda b, p, i: (b, 2 * pairs + p)),
                  _resident((2 * n, 2 * n))],
        out_specs=pl.BlockSpec((n, LANES), lambda b, p, i: (b * nq + i, p)),
        out_shape=jax.ShapeDtypeStruct((t, SB_HEADS * SB_HEAD_DIM), BF16),
        scratch_shapes=[pltpu.VMEM((n, LANES), F32), pltpu.VMEM((n, n), F32)],
        compiler_params=pltpu.CompilerParams(dimension_semantics=("arbitrary", "arbitrary", "arbitrary"),
                                             vmem_limit_bytes=VMEM_LIMIT),
        name="stick_breaking_attention",
    )(qkv, qkv, qkv, tt)


def kernel(x, ssm_norm_w, ssm_in_w, ssm_conv_w, ssm_conv_b, ssm_dt_bias, ssm_a_log, ssm_d, ssm_gate_norm_w, ssm_out_w, kv_norm_w, w_k, w_v, attn_norm_w, w_q, w_o, ffn_norm_w, ffn_up_w, ffn_conv_w, ffn_conv_b, ffn_down_w, final_norm_w):
    bsz, seq, d = x.shape
    h = x.reshape(bsz * seq, d)
    zx, dt = _in_proj(h, seq, ssm_norm_w[0], ssm_in_w[0], ssm_conv_w[0], ssm_conv_b[0], ssm_dt_bias[0])
    y = _ssd(zx, dt, bsz, seq, ssm_a_log[0], ssm_d[0], ssm_gate_norm_w[0])
    h = _proj_residual(y, ssm_out_w[0], h, seq)
    h = _ffn(h, seq, ffn_norm_w[0], ffn_up_w[0], ffn_conv_w[0], ffn_conv_b[0], ffn_down_w[0])
    qkv = _qkv_proj(h, seq, attn_norm_w[0], kv_norm_w, w_q[0], w_k, w_v)
    o = _attention(qkv, bsz, seq)
    h = _proj_residual(o, w_o[0], h, seq)
    h = _ffn(h, seq, ffn_norm_w[1], ffn_up_w[1], ffn_conv_w[1], ffn_conv_b[1], ffn_down_w[1], final_norm_w)
    return h.reshape(bsz, seq, d)
```

```python
import functools
import math

import jax
import jax.numpy as jnp
from jax import lax
from jax.experimental import pallas as pl
from jax.experimental.pallas import tpu as pltpu

D_MODEL = 1024
D_INNER = 2048
SSM_HEADS = 32
SSM_HEAD_DIM = 64
SSM_GROUPS = 4
SSM_STATE = 128
SSM_CONV = 4
SSM_CHUNK = 128
GN = SSM_GROUPS * SSM_STATE
CONV_DIM = D_INNER + 2 * GN
SB_HEADS = 16
SB_HEAD_DIM = 64
SB_BLOCK = 128
D_FF = 2816
FFN_CONV = 3
EPS = 1e-6

LANES = 128
HALO = 16
ROW_TILE = 512
COL_CHUNK = 512
FFN_CHUNK = 256
VMEM_LIMIT = 56 * 1024 * 1024
NEG_BIG = -1e30
ATTN_GROUP = 3
ATTN_STOP = 104.0

F32 = jnp.float32
BF16 = jnp.bfloat16


def _rms(x, w):
    ms = jnp.mean(x * x, axis=-1, keepdims=True)
    return x * lax.rsqrt(ms + EPS) * w


def _softplus(x):
    return jnp.maximum(x, 0.0) + jnp.log(1.0 + jnp.exp(-jnp.abs(x)))


def _silu(x):
    return x / (1.0 + jnp.exp(-x))


def _dot(a, b):
    return jnp.dot(a, b, preferred_element_type=F32)


def _dot_nt(a, b):
    return lax.dot_general(a, b, (((1,), (1,)), ((), ())), preferred_element_type=F32)


def _split3(x):
    h1 = x.astype(BF16)
    r1 = x - h1.astype(F32)
    h2 = r1.astype(BF16)
    h3 = (r1 - h2.astype(F32)).astype(BF16)
    return h1, h2, h3


def _resident(shape):
    return pl.BlockSpec(shape, lambda *_: (0,) * len(shape), pipeline_mode=pl.Buffered(1))


def _fill_normed(xn_ref, x_ref, xh_ref, nw, first):
    xn_ref[HALO:, :] = _rms(x_ref[...], nw).astype(BF16)
    xh = _rms(xh_ref[...], nw)
    xn_ref[:HALO, :] = jnp.where(first, 0.0, xh).astype(BF16)


def _causal_conv(h_ref, cw_ref, cb_ref, lo, width, taps, rows):
    y = cb_ref[:, lo:lo + width]
    for k in range(taps):
        y = y + h_ref[pl.ds(HALO - (taps - 1) + k, rows), :] * cw_ref[k:k + 1, lo:lo + width]
    return y


def _in_proj_kernel(x_ref, xh_ref, nw_ref, w_ref, wdt_ref, cw_ref, cb_ref, dtb_ref,
                    out_ref, dt_ref, xn_ref, acc_ref, *, tm, tiles_per_seq):
    first = (pl.program_id(0) % tiles_per_seq) == 0
    _fill_normed(xn_ref, x_ref, xh_ref, nw_ref[...], first)
    xn = xn_ref[HALO:, :]
    dt_ref[...] = _softplus(_dot(xn, wdt_ref[...]) + dtb_ref[...])
    for c in range(D_INNER // COL_CHUNK):
        lo = c * COL_CHUNK
        out_ref[:, lo:lo + COL_CHUNK] = _dot(xn, w_ref[:, lo:lo + COL_CHUNK]).astype(BF16)
    for c in range(CONV_DIM // COL_CHUNK):
        lo = c * COL_CHUNK
        acc_ref[...] = _dot(xn_ref[...], w_ref[:, D_INNER + lo:D_INNER + lo + COL_CHUNK])
        y = _causal_conv(acc_ref, cw_ref, cb_ref, lo, COL_CHUNK, SSM_CONV, tm)
        out_ref[:, D_INNER + lo:D_INNER + lo + COL_CHUNK] = _silu(y).astype(BF16)


def _row_specs(tm, d):
    per = tm // HALO
    return (pl.BlockSpec((tm, d), lambda i: (i, 0)),
            pl.BlockSpec((HALO, d), lambda i: (jnp.maximum(i * per - 1, 0), 0)))


def _in_proj(h, seq, norm_w, w_in, conv_w, conv_b, dt_bias):
    t, d = h.shape
    tm = min(ROW_TILE, seq)
    w_main = w_in[:, :D_INNER + CONV_DIM].astype(BF16)
    w_dt = jnp.pad(w_in[:, D_INNER + CONV_DIM:], ((0, 0), (0, LANES - SSM_HEADS))).astype(BF16)
    dtb = jnp.pad(dt_bias, (0, LANES - SSM_HEADS)).reshape(1, LANES)
    x_spec, xh_spec = _row_specs(tm, d)
    return pl.pallas_call(
        functools.partial(_in_proj_kernel, tm=tm, tiles_per_seq=seq // tm),
        grid=(t // tm,),
        in_specs=[x_spec, xh_spec, _resident((1, d)), _resident(w_main.shape), _resident(w_dt.shape),
                  _resident(conv_w.shape), _resident((1, CONV_DIM)), _resident((1, LANES))],
        out_specs=[pl.BlockSpec((tm, D_INNER + CONV_DIM), lambda i: (i, 0)),
                   pl.BlockSpec((tm, LANES), lambda i: (i, 0))],
        out_shape=[jax.ShapeDtypeStruct((t, D_INNER + CONV_DIM), BF16),
                   jax.ShapeDtypeStruct((t, LANES), F32)],
        scratch_shapes=[pltpu.VMEM((tm + HALO, d), BF16), pltpu.VMEM((tm + HALO, COL_CHUNK), F32)],
        compiler_params=pltpu.CompilerParams(dimension_semantics=("arbitrary",), vmem_limit_bytes=VMEM_LIMIT),
        name="ssm_in_proj",
    )(h, h, norm_w.reshape(1, d), w_main, w_dt, conv_w, conv_b.reshape(1, CONV_DIM), dtb)


def _pair_cols(arr, h0, low_half):
    return jnp.where(low_half, arr[:, h0:h0 + 1], arr[:, h0 + 1:h0 + 2])


def _ssd_kernel(z_ref, xs_ref, b_ref, c_ref, dt_ref, alog_ref, dskip_ref, gnw_ref,
                y_ref, state_ref, yg_ref):
    @pl.when(pl.program_id(1) == 0)
    def _():
        state_ref[...] = jnp.zeros_like(state_ref)

    n = SSM_CHUNK
    row = lax.broadcasted_iota(jnp.int32, (n, n), 0)
    col = lax.broadcasted_iota(jnp.int32, (n, n), 1)
    causal = col <= row
    low_half = col < SSM_HEAD_DIM
    low_half_row = low_half[:1, :]
    low_half_bf = lax.broadcasted_iota(jnp.int32, (n, n), 1) < SSM_HEAD_DIM

    dt = dt_ref[...]
    ad = dt * (-jnp.exp(alog_ref[...]))
    tri = causal.astype(BF16)
    a1, a2, a3 = _split3(ad)
    acs = _dot(tri, a1) + _dot(tri, a2) + _dot(tri, a3)
    acs_last = acs[n - 1:n, :]
    acs_t = acs.T
    dt_t = dt.T
    e_acs = jnp.exp(acs)
    w_dec = jnp.exp(acs_last - acs) * dt
    c_dec = jnp.exp(acs_last)

    for g in range(SSM_GROUPS):
        gl = g * SSM_STATE
        bg = b_ref[:, gl:gl + SSM_STATE]
        cg = c_ref[:, gl:gl + SSM_STATE]
        cb = _dot_nt(cg, bg)
        bg_t = bg.astype(F32).T.astype(BF16)
        for p in range(SSM_HEADS // SSM_GROUPS // 2):
            h0 = g * (SSM_HEADS // SSM_GROUPS) + 2 * p
            lo = h0 * SSM_HEAD_DIM
            xs_pair = xs_ref[:, lo:lo + LANES]
            xs_f = xs_pair.astype(F32)
            ms = []
            for h in (h0, h0 + 1):
                diff = acs[:, h:h + 1] - acs_t[h:h + 1, :]
                lmat = jnp.exp(jnp.where(causal, diff, NEG_BIG))
                ms.append((cb * lmat * dt_t[h:h + 1, :]).astype(BF16))
            zero = jnp.zeros_like(xs_pair)
            rhs = jnp.concatenate([jnp.where(low_half_bf, xs_pair, zero),
                                   jnp.where(low_half_bf, zero, xs_pair)], axis=0)
            y_diag = _dot(jnp.concatenate(ms, axis=1), rhs)
            s_pair = state_ref[:, lo:lo + LANES]
            y_off = _dot(cg, s_pair.astype(BF16)) * _pair_cols(e_acs, h0, low_half)
            xw = (xs_f * _pair_cols(w_dec, h0, low_half)).astype(BF16)
            state_ref[:, lo:lo + LANES] = s_pair * _pair_cols(c_dec, h0, low_half_row) + _dot(bg_t, xw)
            y = y_diag + y_off + dskip_ref[:, lo:lo + LANES] * xs_f
            yg_ref[:, lo:lo + LANES] = y * _silu(z_ref[:, lo:lo + LANES].astype(F32))
        gw = D_INNER // SSM_GROUPS
        yg = yg_ref[:, g * gw:(g + 1) * gw]
        y_ref[:, g * gw:(g + 1) * gw] = _rms(yg, gnw_ref[:, g * gw:(g + 1) * gw]).astype(BF16)


def _ssd(zx, dt, bsz, seq, a_log, d_skip, gate_norm_w):
    t = zx.shape[0]
    nc = seq // SSM_CHUNK
    alog = jnp.pad(a_log, (0, LANES - SSM_HEADS)).reshape(1, LANES)
    dskip = jnp.repeat(d_skip, SSM_HEAD_DIM).reshape(1, D_INNER)
    rowblk = lambda b, c: b * nc + c
    x_blk = D_INNER // D_INNER
    return pl.pallas_call(
        _ssd_kernel,
        grid=(bsz, nc),
        in_specs=[pl.BlockSpec((SSM_CHUNK, D_INNER), lambda b, c: (rowblk(b, c), 0)),
                  pl.BlockSpec((SSM_CHUNK, D_INNER), lambda b, c: (rowblk(b, c), x_blk)),
                  pl.BlockSpec((SSM_CHUNK, GN), lambda b, c: (rowblk(b, c), 2 * D_INNER // GN)),
                  pl.BlockSpec((SSM_CHUNK, GN), lambda b, c: (rowblk(b, c), 2 * D_INNER // GN + 1)),
                  pl.BlockSpec((SSM_CHUNK, LANES), lambda b, c: (rowblk(b, c), 0)),
                  _resident((1, LANES)), _resident((1, D_INNER)), _resident((1, D_INNER))],
        out_specs=pl.BlockSpec((SSM_CHUNK, D_INNER), lambda b, c: (rowblk(b, c), 0)),
        out_shape=jax.ShapeDtypeStruct((t, D_INNER), BF16),
        scratch_shapes=[pltpu.VMEM((SSM_STATE, D_INNER), F32), pltpu.VMEM((SSM_CHUNK, D_INNER), F32)],
        compiler_params=pltpu.CompilerParams(dimension_semantics=("arbitrary", "arbitrary"),
                                             vmem_limit_bytes=VMEM_LIMIT),
        name="ssd_scan",
    )(zx, zx, zx, zx, dt, alog, dskip, gate_norm_w.reshape(1, D_INNER))


def _proj_residual_kernel(a_ref, w_ref, res_ref, out_ref):
    for c in range(D_MODEL // COL_CHUNK):
        lo = c * COL_CHUNK
        out_ref[:, lo:lo + COL_CHUNK] = res_ref[:, lo:lo + COL_CHUNK] + _dot(a_ref[...], w_ref[:, lo:lo + COL_CHUNK])


def _proj_residual(a, w, res, seq):
    t, k = a.shape
    tm = min(ROW_TILE, seq)
    return pl.pallas_call(
        _proj_residual_kernel,
        grid=(t // tm,),
        in_specs=[pl.BlockSpec((tm, k), lambda i: (i, 0)), _resident((k, D_MODEL)),
                  pl.BlockSpec((tm, D_MODEL), lambda i: (i, 0))],
        out_specs=pl.BlockSpec((tm, D_MODEL), lambda i: (i, 0)),
        out_shape=jax.ShapeDtypeStruct((t, D_MODEL), F32),
        compiler_params=pltpu.CompilerParams(dimension_semantics=("arbitrary",), vmem_limit_bytes=VMEM_LIMIT),
        name="proj_residual",
    )(a, w.astype(BF16), res)


def _ffn_kernel(x_ref, xh_ref, nw_ref, wup_ref, cw_ref, cb_ref, wd_ref, fnw_ref,
                out_ref, xn_ref, h_ref, acc_ref, *, tm, tiles_per_seq, final_norm):
    first = (pl.program_id(0) % tiles_per_seq) == 0
    _fill_normed(xn_ref, x_ref, xh_ref, nw_ref[...], first)
    fc = FFN_CHUNK
    for c in range(D_FF // fc):
        lo = 2 * fc * c
        h_ref[...] = _dot(xn_ref[...], wup_ref[:, lo:lo + 2 * fc])
        gv = _causal_conv(h_ref, cw_ref, cb_ref, lo, 2 * fc, FFN_CONV, tm)
        hid = (_silu(gv[:, :fc]) * gv[:, fc:]).astype(BF16)
        contrib = _dot(hid, wd_ref[c * fc:(c + 1) * fc, :])
        if c == 0:
            acc_ref[...] = contrib
        else:
            acc_ref[...] += contrib
    out = x_ref[...] + acc_ref[...]
    if final_norm:
        out = _rms(out, fnw_ref[...])
    out_ref[...] = out


def _interleave_gate_value(a):
    lead = a.shape[:-1]
    g = a[..., :D_FF].reshape(*lead, D_FF // FFN_CHUNK, FFN_CHUNK)
    v = a[..., D_FF:].reshape(*lead, D_FF // FFN_CHUNK, FFN_CHUNK)
    return jnp.stack([g, v], axis=-2).reshape(*lead, 2 * D_FF)


def _ffn(h, seq, norm_w, w_up, conv_w, conv_b, w_down, final_norm_w=None):
    t, d = h.shape
    tm = min(ROW_TILE, seq)
    final_norm = final_norm_w is not None
    fnw = (final_norm_w if final_norm else jnp.ones((d,), F32)).reshape(1, d)
    x_spec, xh_spec = _row_specs(tm, d)
    return pl.pallas_call(
        functools.partial(_ffn_kernel, tm=tm, tiles_per_seq=seq // tm, final_norm=final_norm),
        grid=(t // tm,),
        in_specs=[x_spec, xh_spec, _resident((1, d)), _resident((d, 2 * D_FF)), _resident((FFN_CONV, 2 * D_FF)),
                  _resident((1, 2 * D_FF)), _resident((D_FF, d)), _resident((1, d))],
        out_specs=pl.BlockSpec((tm, d), lambda i: (i, 0)),
        out_shape=jax.ShapeDtypeStruct((t, d), F32),
        scratch_shapes=[pltpu.VMEM((tm + HALO, d), BF16), pltpu.VMEM((tm + HALO, 2 * FFN_CHUNK), F32),
                        pltpu.VMEM((tm, d), F32)],
        compiler_params=pltpu.CompilerParams(dimension_semantics=("arbitrary",), vmem_limit_bytes=VMEM_LIMIT),
        name="conv_ffn",
    )(h, h, norm_w.reshape(1, d), _interleave_gate_value(w_up).astype(BF16), _interleave_gate_value(conv_w),
      _interleave_gate_value(conv_b).reshape(1, 2 * D_FF), w_down.astype(BF16), fnw)


def _qkv_kernel(x_ref, qnw_ref, kvnw_ref, w_ref, out_ref):
    x = x_ref[...]
    xhat = x * lax.rsqrt(jnp.mean(x * x, axis=-1, keepdims=True) + EPS)
    xq = (xhat * qnw_ref[...]).astype(BF16)
    xkv = (xhat * kvnw_ref[...]).astype(BF16)
    for c in range(3 * D_MODEL // COL_CHUNK):
        lo = c * COL_CHUNK
        src = xq if lo < D_MODEL else xkv
        out_ref[:, lo:lo + COL_CHUNK] = _dot(src, w_ref[:, lo:lo + COL_CHUNK]).astype(BF16)


def _qkv_proj(h, seq, q_norm_w, kv_norm_w, w_q, w_k, w_v):
    t, d = h.shape
    tm = min(ROW_TILE, seq)
    w = jnp.concatenate([w_q, w_k, w_v], axis=1).astype(BF16)
    return pl.pallas_call(
        _qkv_kernel,
        grid=(t // tm,),
        in_specs=[pl.BlockSpec((tm, d), lambda i: (i, 0)), _resident((1, d)), _resident((1, d)),
                  _resident((d, 3 * d))],
        out_specs=pl.BlockSpec((tm, 3 * d), lambda i: (i, 0)),
        out_shape=jax.ShapeDtypeStruct((t, 3 * d), BF16),
        compiler_params=pltpu.CompilerParams(dimension_semantics=("arbitrary",), vmem_limit_bytes=VMEM_LIMIT),
        name="qkv_proj",
    )(h, q_norm_w.reshape(1, d), kv_norm_w.reshape(1, d), w)


def _attn_kernel(q_ref, k_ref, v_ref, tt_ref, out_ref, ks_ref, vs_ref, acc_ref, carry_ref):
    i = pl.program_id(2)
    n = SB_BLOCK
    nblk = ks_ref.shape[0]
    row = lax.broadcasted_iota(jnp.int32, (n, n), 0)
    col = lax.broadcasted_iota(jnp.int32, (n, n), 1)
    low_half = col < SB_HEAD_DIM
    scale = 1.0 / math.sqrt(SB_HEAD_DIM)

    @pl.when(i == 0)
    def _():
        def fill(j, _):
            sl = pl.ds(pl.multiple_of(j * n, n), n)
            kj = k_ref[sl, :]
            vj = v_ref[sl, :]
            zero = jnp.zeros_like(kj)
            ks_ref[j] = jnp.concatenate([jnp.where(low_half, kj, zero), jnp.where(low_half, zero, kj)], axis=0)
            vs_ref[j] = jnp.concatenate([jnp.where(low_half, vj, zero), jnp.where(low_half, zero, vj)], axis=0)
            return 0
        lax.fori_loop(0, nblk, fill, 0)

    q = q_ref[...]
    tt = tt_ref[...]
    acc_ref[...] = jnp.zeros_like(acc_ref)
    carry_ref[...] = jnp.zeros_like(carry_ref)
    ngroups = (i + ATTN_GROUP) // ATTN_GROUP

    def group(state):
        t, _ = state
        carries = [carry_ref[0], carry_ref[1]]
        acc = acc_ref[...]
        for g in range(ATTN_GROUP):
            j = i - (t * ATTN_GROUP + g)
            jc = jnp.maximum(j, 0)
            mask = jnp.logical_and(j >= 0, (jc * n + col) < (i * n + row))
            z2 = _dot_nt(q, ks_ref[jc]) * scale
            sp2 = _softplus(z2)
            atts = []
            for h in range(2):
                z = z2[:, h * n:(h + 1) * n]
                sp = sp2[:, h * n:(h + 1) * n]
                spm = jnp.where(mask, sp, 0.0)
                hi = spm.astype(BF16)
                lo = (spm - hi.astype(F32)).astype(BF16)
                cs = _dot(jnp.concatenate([hi, lo], axis=1), tt)
                later = cs[:, :n] + carries[h]
                atts.append(jnp.where(mask, jnp.exp(z - sp - later), 0.0).astype(BF16))
                carries[h] = carries[h] + cs[:, n:]
            acc = acc + _dot(jnp.concatenate(atts, axis=1), vs_ref[jc])
        acc_ref[...] = acc
        carry_ref[0] = carries[0]
        carry_ref[1] = carries[1]
        return t + 1, jnp.min(jnp.minimum(carries[0], carries[1]))

    def more(state):
        t, min_carry = state
        return jnp.logical_and(t < ngroups, min_carry <= ATTN_STOP)

    lax.while_loop(more, group, (jnp.int32(0), jnp.float32(0.0)))
    out_ref[...] = acc_ref[...].astype(BF16)


def _attention(qkv, bsz, seq):
    t = qkv.shape[0]
    n = SB_BLOCK
    nq = seq // n
    pairs = SB_HEADS // 2
    r = jnp.arange(n)
    tri = (r[:, None] > r[None, :]).astype(BF16)
    blk = jnp.concatenate([tri, jnp.ones((n, n), BF16)], axis=1)
    tt = jnp.concatenate([blk, blk], axis=0)
    return pl.pallas_call(
        _attn_kernel,
        grid=(bsz, pairs, nq),
        in_specs=[pl.BlockSpec((n, LANES), lambda b, p, i: (b * nq + i, p)),
                  pl.BlockSpec((seq, LANES), lambda b, p, i: (b, pairs + p)),
                  pl.BlockSpec((seq, LANES), lambda b, p, i: (b, 2 * pairs + p)),
                  _resident((2 * n, 2 * n))],
        out_specs=pl.BlockSpec((n, LANES), lambda b, p, i: (b * nq + i, p)),
        out_shape=jax.ShapeDtypeStruct((t, SB_HEADS * SB_HEAD_DIM), BF16),
        scratch_shapes=[pltpu.VMEM((nq, 2 * n, LANES), BF16), pltpu.VMEM((nq, 2 * n, LANES), BF16),
                        pltpu.VMEM((n, LANES), F32), pltpu.VMEM((2, n, n), F32)],
        compiler_params=pltpu.CompilerParams(dimension_semantics=("arbitrary", "arbitrary", "arbitrary"),
                                             vmem_limit_bytes=VMEM_LIMIT),
        name="stick_breaking_attention",
    )(qkv, qkv, qkv, tt)


def kernel(x, ssm_norm_w, ssm_in_w, ssm_conv_w, ssm_conv_b, ssm_dt_bias, ssm_a_log, ssm_d, ssm_gate_norm_w, ssm_out_w, kv_norm_w, w_k, w_v, attn_norm_w, w_q, w_o, ffn_norm_w, ffn_up_w, ffn_conv_w, ffn_conv_b, ffn_down_w, final_norm_w):
    bsz, seq, d = x.shape
    h = x.reshape(bsz * seq, d)
    zx, dt = _in_proj(h, seq, ssm_norm_w[0], ssm_in_w[0], ssm_conv_w[0], ssm_conv_b[0], ssm_dt_bias[0])
    y = _ssd(zx, dt, bsz, seq, ssm_a_log[0], ssm_d[0], ssm_gate_norm_w[0])
    h = _proj_residual(y, ssm_out_w[0], h, seq)
    h = _ffn(h, seq, ffn_norm_w[0], ffn_up_w[0], ffn_conv_w[0], ffn_conv_b[0], ffn_down_w[0])
    qkv = _qkv_proj(h, seq, attn_norm_w[0], kv_norm_w, w_q[0], w_k, w_v)
    o = _attention(qkv, bsz, seq)
    h = _proj_residual(o, w_o[0], h, seq)
    h = _ffn(h, seq, ffn_norm_w[1], ffn_up_w[1], ffn_conv_w[1], ffn_conv_b[1], ffn_down_w[1], final_norm_w)
    return h.reshape(bsz, seq, d)
```

```python
import functools
import math

import jax
import jax.numpy as jnp
from jax import lax
from jax.experimental import pallas as pl
from jax.experimental.pallas import tpu as pltpu

D_MODEL = 1024
D_INNER = 2048
SSM_HEADS = 32
SSM_HEAD_DIM = 64
SSM_GROUPS = 4
SSM_STATE = 128
SSM_CONV = 4
SSM_CHUNK = 128
GN = SSM_GROUPS * SSM_STATE
CONV_DIM = D_INNER + 2 * GN
SB_HEADS = 16
SB_HEAD_DIM = 64
SB_BLOCK = 128
D_FF = 2816
FFN_CONV = 3
EPS = 1e-6

LANES = 128
HALO = 16
ROW_TILE = 512
COL_CHUNK = 512
FFN_CHUNK = 256
VMEM_LIMIT = 56 * 1024 * 1024
NEG_BIG = -1e30
ATTN_GROUP = 3
ATTN_TILES = 4
ATTN_STOP = 104.0

F32 = jnp.float32
BF16 = jnp.bfloat16


def _rms(x, w):
    ms = jnp.mean(x * x, axis=-1, keepdims=True)
    return x * lax.rsqrt(ms + EPS) * w


def _softplus(x):
    return jnp.maximum(x, 0.0) + jnp.log(1.0 + jnp.exp(-jnp.abs(x)))


def _silu(x):
    return x / (1.0 + jnp.exp(-x))


def _dot(a, b):
    return jnp.dot(a, b, preferred_element_type=F32)


def _dot_nt(a, b):
    return lax.dot_general(a, b, (((1,), (1,)), ((), ())), preferred_element_type=F32)


def _split3(x):
    h1 = x.astype(BF16)
    r1 = x - h1.astype(F32)
    h2 = r1.astype(BF16)
    h3 = (r1 - h2.astype(F32)).astype(BF16)
    return h1, h2, h3


def _resident(shape):
    return pl.BlockSpec(shape, lambda *_: (0,) * len(shape), pipeline_mode=pl.Buffered(1))


def _fill_normed(xn_ref, x_ref, xh_ref, nw, first):
    xn_ref[HALO:, :] = _rms(x_ref[...], nw).astype(BF16)
    xh = _rms(xh_ref[...], nw)
    xn_ref[:HALO, :] = jnp.where(first, 0.0, xh).astype(BF16)


def _causal_conv(h_ref, cw_ref, cb_ref, lo, width, taps, rows):
    y = cb_ref[:, lo:lo + width]
    for k in range(taps):
        y = y + h_ref[pl.ds(HALO - (taps - 1) + k, rows), :] * cw_ref[k:k + 1, lo:lo + width]
    return y


def _in_proj_kernel(x_ref, xh_ref, nw_ref, w_ref, wdt_ref, cw_ref, cb_ref, dtb_ref,
                    out_ref, dt_ref, xn_ref, acc_ref, *, tm, tiles_per_seq):
    first = (pl.program_id(0) % tiles_per_seq) == 0
    _fill_normed(xn_ref, x_ref, xh_ref, nw_ref[...], first)
    xn = xn_ref[HALO:, :]
    dt_ref[...] = _softplus(_dot(xn, wdt_ref[...]) + dtb_ref[...])
    for c in range(D_INNER // COL_CHUNK):
        lo = c * COL_CHUNK
        out_ref[:, lo:lo + COL_CHUNK] = _dot(xn, w_ref[:, lo:lo + COL_CHUNK]).astype(BF16)
    nchunks = CONV_DIM // COL_CHUNK

    def xbc_proj(c):
        lo = D_INNER + c * COL_CHUNK
        acc_ref[c % 2] = _dot(xn_ref[...], w_ref[:, lo:lo + COL_CHUNK])

    xbc_proj(0)
    for c in range(nchunks):
        lo = c * COL_CHUNK
        if c + 1 < nchunks:
            xbc_proj(c + 1)
        y = _causal_conv(acc_ref.at[c % 2], cw_ref, cb_ref, lo, COL_CHUNK, SSM_CONV, tm)
        out_ref[:, D_INNER + lo:D_INNER + lo + COL_CHUNK] = _silu(y).astype(BF16)


def _row_specs(tm, d):
    per = tm // HALO
    return (pl.BlockSpec((tm, d), lambda i: (i, 0)),
            pl.BlockSpec((HALO, d), lambda i: (jnp.maximum(i * per - 1, 0), 0)))


def _in_proj(h, seq, norm_w, w_in, conv_w, conv_b, dt_bias):
    t, d = h.shape
    tm = min(ROW_TILE, seq)
    w_main = w_in[:, :D_INNER + CONV_DIM].astype(BF16)
    w_dt = jnp.pad(w_in[:, D_INNER + CONV_DIM:], ((0, 0), (0, LANES - SSM_HEADS))).astype(BF16)
    dtb = jnp.pad(dt_bias, (0, LANES - SSM_HEADS)).reshape(1, LANES)
    x_spec, xh_spec = _row_specs(tm, d)
    return pl.pallas_call(
        functools.partial(_in_proj_kernel, tm=tm, tiles_per_seq=seq // tm),
        grid=(t // tm,),
        in_specs=[x_spec, xh_spec, _resident((1, d)), _resident(w_main.shape), _resident(w_dt.shape),
                  _resident(conv_w.shape), _resident((1, CONV_DIM)), _resident((1, LANES))],
        out_specs=[pl.BlockSpec((tm, D_INNER + CONV_DIM), lambda i: (i, 0)),
                   pl.BlockSpec((tm, LANES), lambda i: (i, 0))],
        out_shape=[jax.ShapeDtypeStruct((t, D_INNER + CONV_DIM), BF16),
                   jax.ShapeDtypeStruct((t, LANES), F32)],
        scratch_shapes=[pltpu.VMEM((tm + HALO, d), BF16), pltpu.VMEM((2, tm + HALO, COL_CHUNK), F32)],
        compiler_params=pltpu.CompilerParams(dimension_semantics=("arbitrary",), vmem_limit_bytes=VMEM_LIMIT),
        name="ssm_in_proj",
    )(h, h, norm_w.reshape(1, d), w_main, w_dt, conv_w, conv_b.reshape(1, CONV_DIM), dtb)


def _pair_cols(arr, h0, low_half):
    return jnp.where(low_half, arr[:, h0:h0 + 1], arr[:, h0 + 1:h0 + 2])


def _ssd_kernel(z_ref, xs_ref, b_ref, c_ref, dt_ref, alog_ref, dskip_ref, gnw_ref,
                y_ref, state_ref, yg_ref):
    @pl.when(pl.program_id(1) == 0)
    def _():
        state_ref[...] = jnp.zeros_like(state_ref)

    n = SSM_CHUNK
    row = lax.broadcasted_iota(jnp.int32, (n, n), 0)
    col = lax.broadcasted_iota(jnp.int32, (n, n), 1)
    causal = col <= row
    low_half = col < SSM_HEAD_DIM
    low_half_row = low_half[:1, :]
    low_half_bf = lax.broadcasted_iota(jnp.int32, (n, n), 1) < SSM_HEAD_DIM

    dt = dt_ref[...]
    ad = dt * (-jnp.exp(alog_ref[...]))
    tri = causal.astype(BF16)
    a1, a2, a3 = _split3(ad)
    acs = _dot(tri, a1) + _dot(tri, a2) + _dot(tri, a3)
    acs_last = acs[n - 1:n, :]
    acs_t = acs.T
    dt_t = dt.T
    e_acs = jnp.exp(acs)
    w_dec = jnp.exp(acs_last - acs) * dt
    c_dec = jnp.exp(acs_last)

    for g in range(SSM_GROUPS):
        gl = g * SSM_STATE
        bg = b_ref[:, gl:gl + SSM_STATE]
        cg = c_ref[:, gl:gl + SSM_STATE]
        cb = _dot_nt(cg, bg)
        bg_t = bg.astype(F32).T.astype(BF16)
        for p in range(SSM_HEADS // SSM_GROUPS // 2):
            h0 = g * (SSM_HEADS // SSM_GROUPS) + 2 * p
            lo = h0 * SSM_HEAD_DIM
            xs_pair = xs_ref[:, lo:lo + LANES]
            xs_f = xs_pair.astype(F32)
            ms = []
            for h in (h0, h0 + 1):
                diff = acs[:, h:h + 1] - acs_t[h:h + 1, :]
                lmat = jnp.exp(jnp.where(causal, diff, NEG_BIG))
                ms.append((cb * lmat * dt_t[h:h + 1, :]).astype(BF16))
            zero = jnp.zeros_like(xs_pair)
            rhs = jnp.concatenate([jnp.where(low_half_bf, xs_pair, zero),
                                   jnp.where(low_half_bf, zero, xs_pair)], axis=0)
            y_diag = _dot(jnp.concatenate(ms, axis=1), rhs)
            s_pair = state_ref[:, lo:lo + LANES]
            y_off = _dot(cg, s_pair.astype(BF16)) * _pair_cols(e_acs, h0, low_half)
            xw = (xs_f * _pair_cols(w_dec, h0, low_half)).astype(BF16)
            state_ref[:, lo:lo + LANES] = s_pair * _pair_cols(c_dec, h0, low_half_row) + _dot(bg_t, xw)
            y = y_diag + y_off + dskip_ref[:, lo:lo + LANES] * xs_f
            yg_ref[:, lo:lo + LANES] = y * _silu(z_ref[:, lo:lo + LANES].astype(F32))
        gw = D_INNER // SSM_GROUPS
        yg = yg_ref[:, g * gw:(g + 1) * gw]
        y_ref[:, g * gw:(g + 1) * gw] = _rms(yg, gnw_ref[:, g * gw:(g + 1) * gw]).astype(BF16)


def _ssd(zx, dt, bsz, seq, a_log, d_skip, gate_norm_w):
    t = zx.shape[0]
    nc = seq // SSM_CHUNK
    alog = jnp.pad(a_log, (0, LANES - SSM_HEADS)).reshape(1, LANES)
    dskip = jnp.repeat(d_skip, SSM_HEAD_DIM).reshape(1, D_INNER)
    rowblk = lambda b, c: b * nc + c
    x_blk = D_INNER // D_INNER
    return pl.pallas_call(
        _ssd_kernel,
        grid=(bsz, nc),
        in_specs=[pl.BlockSpec((SSM_CHUNK, D_INNER), lambda b, c: (rowblk(b, c), 0)),
                  pl.BlockSpec((SSM_CHUNK, D_INNER), lambda b, c: (rowblk(b, c), x_blk)),
                  pl.BlockSpec((SSM_CHUNK, GN), lambda b, c: (rowblk(b, c), 2 * D_INNER // GN)),
                  pl.BlockSpec((SSM_CHUNK, GN), lambda b, c: (rowblk(b, c), 2 * D_INNER // GN + 1)),
                  pl.BlockSpec((SSM_CHUNK, LANES), lambda b, c: (rowblk(b, c), 0)),
                  _resident((1, LANES)), _resident((1, D_INNER)), _resident((1, D_INNER))],
        out_specs=pl.BlockSpec((SSM_CHUNK, D_INNER), lambda b, c: (rowblk(b, c), 0)),
        out_shape=jax.ShapeDtypeStruct((t, D_INNER), BF16),
        scratch_shapes=[pltpu.VMEM((SSM_STATE, D_INNER), F32), pltpu.VMEM((SSM_CHUNK, D_INNER), F32)],
        compiler_params=pltpu.CompilerParams(dimension_semantics=("arbitrary", "arbitrary"),
                                             vmem_limit_bytes=VMEM_LIMIT),
        name="ssd_scan",
    )(zx, zx, zx, zx, dt, alog, dskip, gate_norm_w.reshape(1, D_INNER))


def _proj_residual_kernel(a_ref, w_ref, res_ref, out_ref):
    for c in range(D_MODEL // COL_CHUNK):
        lo = c * COL_CHUNK
        out_ref[:, lo:lo + COL_CHUNK] = res_ref[:, lo:lo + COL_CHUNK] + _dot(a_ref[...], w_ref[:, lo:lo + COL_CHUNK])


def _proj_residual(a, w, res, seq):
    t, k = a.shape
    tm = min(ROW_TILE, seq)
    return pl.pallas_call(
        _proj_residual_kernel,
        grid=(t // tm,),
        in_specs=[pl.BlockSpec((tm, k), lambda i: (i, 0)), _resident((k, D_MODEL)),
                  pl.BlockSpec((tm, D_MODEL), lambda i: (i, 0))],
        out_specs=pl.BlockSpec((tm, D_MODEL), lambda i: (i, 0)),
        out_shape=jax.ShapeDtypeStruct((t, D_MODEL), F32),
        compiler_params=pltpu.CompilerParams(dimension_semantics=("arbitrary",), vmem_limit_bytes=VMEM_LIMIT),
        name="proj_residual",
    )(a, w.astype(BF16), res)


def _ffn_kernel(x_ref, xh_ref, nw_ref, wup_ref, cw_ref, cb_ref, wd_ref, fnw_ref,
                out_ref, xn_ref, h_ref, acc_ref, *, tm, tiles_per_seq, final_norm):
    first = (pl.program_id(0) % tiles_per_seq) == 0
    _fill_normed(xn_ref, x_ref, xh_ref, nw_ref[...], first)
    fc = FFN_CHUNK
    nchunks = D_FF // fc

    def up_proj(c):
        h_ref[c % 2] = _dot(xn_ref[...], wup_ref[:, 2 * fc * c:2 * fc * (c + 1)])

    up_proj(0)
    for c in range(nchunks):
        lo = 2 * fc * c
        if c + 1 < nchunks:
            up_proj(c + 1)
        gv = _causal_conv(h_ref.at[c % 2], cw_ref, cb_ref, lo, 2 * fc, FFN_CONV, tm)
        hid = (_silu(gv[:, :fc]) * gv[:, fc:]).astype(BF16)
        contrib = _dot(hid, wd_ref[c * fc:(c + 1) * fc, :])
        if c == 0:
            acc_ref[...] = contrib
        else:
            acc_ref[...] += contrib
    out = x_ref[...] + acc_ref[...]
    if final_norm:
        out = _rms(out, fnw_ref[...])
    out_ref[...] = out


def _interleave_gate_value(a):
    lead = a.shape[:-1]
    g = a[..., :D_FF].reshape(*lead, D_FF // FFN_CHUNK, FFN_CHUNK)
    v = a[..., D_FF:].reshape(*lead, D_FF // FFN_CHUNK, FFN_CHUNK)
    return jnp.stack([g, v], axis=-2).reshape(*lead, 2 * D_FF)


def _ffn(h, seq, norm_w, w_up, conv_w, conv_b, w_down, final_norm_w=None):
    t, d = h.shape
    tm = min(ROW_TILE, seq)
    final_norm = final_norm_w is not None
    fnw = (final_norm_w if final_norm else jnp.ones((d,), F32)).reshape(1, d)
    x_spec, xh_spec = _row_specs(tm, d)
    return pl.pallas_call(
        functools.partial(_ffn_kernel, tm=tm, tiles_per_seq=seq // tm, final_norm=final_norm),
        grid=(t // tm,),
        in_specs=[x_spec, xh_spec, _resident((1, d)), _resident((d, 2 * D_FF)), _resident((FFN_CONV, 2 * D_FF)),
                  _resident((1, 2 * D_FF)), _resident((D_FF, d)), _resident((1, d))],
        out_specs=pl.BlockSpec((tm, d), lambda i: (i, 0)),
        out_shape=jax.ShapeDtypeStruct((t, d), F32),
        scratch_shapes=[pltpu.VMEM((tm + HALO, d), BF16), pltpu.VMEM((2, tm + HALO, 2 * FFN_CHUNK), F32),
                        pltpu.VMEM((tm, d), F32)],
        compiler_params=pltpu.CompilerParams(dimension_semantics=("arbitrary",), vmem_limit_bytes=VMEM_LIMIT),
        name="conv_ffn",
    )(h, h, norm_w.reshape(1, d), _interleave_gate_value(w_up.astype(BF16)), _interleave_gate_value(conv_w),
      _interleave_gate_value(conv_b).reshape(1, 2 * D_FF), w_down.astype(BF16), fnw)


def _qkv_kernel(x_ref, qnw_ref, kvnw_ref, w_ref, out_ref):
    x = x_ref[...]
    xhat = x * lax.rsqrt(jnp.mean(x * x, axis=-1, keepdims=True) + EPS)
    xq = (xhat * qnw_ref[...]).astype(BF16)
    xkv = (xhat * kvnw_ref[...]).astype(BF16)
    for c in range(3 * D_MODEL // COL_CHUNK):
        lo = c * COL_CHUNK
        src = xq if lo < D_MODEL else xkv
        out_ref[:, lo:lo + COL_CHUNK] = _dot(src, w_ref[:, lo:lo + COL_CHUNK]).astype(BF16)


def _qkv_proj(h, seq, q_norm_w, kv_norm_w, w_q, w_k, w_v):
    t, d = h.shape
    tm = min(ROW_TILE, seq)
    w = jnp.concatenate([w_q, w_k, w_v], axis=1).astype(BF16)
    return pl.pallas_call(
        _qkv_kernel,
        grid=(t // tm,),
        in_specs=[pl.BlockSpec((tm, d), lambda i: (i, 0)), _resident((1, d)), _resident((1, d)),
                  _resident((d, 3 * d))],
        out_specs=pl.BlockSpec((tm, 3 * d), lambda i: (i, 0)),
        out_shape=jax.ShapeDtypeStruct((t, 3 * d), BF16),
        compiler_params=pltpu.CompilerParams(dimension_semantics=("arbitrary",), vmem_limit_bytes=VMEM_LIMIT),
        name="qkv_proj",
    )(h, q_norm_w.reshape(1, d), kv_norm_w.reshape(1, d), w)


def _attn_kernel(q_ref, k_ref, v_ref, tt_ref, out_ref, ks_ref, vs_ref):
    i = pl.program_id(2)
    n = SB_BLOCK
    nblk = ks_ref.shape[0] - 1
    row = lax.broadcasted_iota(jnp.int32, (n, n), 0)
    col = lax.broadcasted_iota(jnp.int32, (n, n), 1)
    low_half = col < SB_HEAD_DIM
    before = col < row

    @pl.when(i == 0)
    def _():
        def fill(j, _):
            sl = pl.ds(pl.multiple_of(j * n, n), n)
            kj = k_ref[sl, :]
            vj = v_ref[sl, :]
            zero = jnp.zeros_like(kj)
            ks_ref[j] = jnp.concatenate([jnp.where(low_half, kj, zero), jnp.where(low_half, zero, kj)], axis=0)
            vs_ref[j] = jnp.concatenate([jnp.where(low_half, vj, zero), jnp.where(low_half, zero, vj)], axis=0)
            return 0
        lax.fori_loop(0, nblk, fill, 0)
        ks_ref[nblk] = jnp.zeros(ks_ref.shape[1:], BF16)
        vs_ref[nblk] = jnp.zeros(vs_ref.shape[1:], BF16)

    tt = tt_ref[...]
    scale = 1.0 / math.sqrt(SB_HEAD_DIM)
    tiles = [i * ATTN_TILES + u for u in range(ATTN_TILES)]
    qss = [q_ref[u * n:(u + 1) * n, :] * scale for u in range(ATTN_TILES)]

    def sweep(us, ts, states, diagonal):
        work = [(u, g) for u in range(len(us)) for g in range(ATTN_GROUP)]
        jcs, z2s, sp2s, laters = {}, {}, {}, {}
        for u, g in work:
            j = tiles[us[u]] - (ts[u] * ATTN_GROUP + g)
            jcs[u, g] = jnp.where(j >= 0, j, nblk)
            z2s[u, g] = _dot_nt(qss[us[u]], ks_ref[jcs[u, g]])
        for key in work:
            sp2s[key] = _softplus(z2s[key])
        carries = [list(st[0]) for st in states]
        for u, g in work:
            for h in range(2):
                sp = sp2s[u, g][:, h * n:(h + 1) * n]
                if diagonal and g == 0:
                    sp = jnp.where(before, sp, 0.0)
                cs = _dot(sp.astype(BF16), tt)
                if carries[u][h] is None:
                    laters[u, g, h], carries[u][h] = cs[:, :n], cs[:, n:]
                else:
                    laters[u, g, h] = cs[:, :n] + carries[u][h]
                    carries[u][h] = carries[u][h] + cs[:, n:]
        mins = [jnp.min(jnp.minimum(c[0], c[1])) for c in carries]
        accs = [st[1] for st in states]
        for u, g in work:
            atts = []
            for h in range(2):
                sl = slice(h * n, (h + 1) * n)
                att = jnp.exp(z2s[u, g][:, sl] - sp2s[u, g][:, sl] - laters[u, g, h])
                if diagonal and g == 0:
                    att = jnp.where(before, att, 0.0)
                atts.append(att.astype(BF16))
            pv = _dot(jnp.concatenate(atts, axis=1), vs_ref[jcs[u, g]])
            accs[u] = pv if accs[u] is None else accs[u] + pv
        return [(carries[u], accs[u], mins[u]) for u in range(len(us))]

    us = list(range(ATTN_TILES))
    first = sweep(us, [0] * ATTN_TILES, [([None, None], None)] * ATTN_TILES, True)
    for u in us:
        carries, acc, min_carry = first[u]
        ngroups = (tiles[u] + ATTN_GROUP) // ATTN_GROUP

        def more(state, ngroups=ngroups):
            return jnp.logical_and(state[0] < ngroups, state[1] <= ATTN_STOP)

        def group(state, u=u):
            t, _, c0, c1, acc = state
            (carries, acc, min_carry), = sweep([u], [t], [([c0, c1], acc)], False)
            return t + 1, min_carry, carries[0], carries[1], acc

        state = lax.while_loop(more, group, (jnp.int32(1), min_carry, carries[0], carries[1], acc))
        out_ref[u * n:(u + 1) * n, :] = state[4].astype(BF16)


def _attention(qkv, bsz, seq):
    t = qkv.shape[0]
    n = SB_BLOCK
    nq = seq // n
    pairs = SB_HEADS // 2
    r = jnp.arange(n)
    tri = (r[:, None] > r[None, :]).astype(BF16)
    tt = jnp.concatenate([tri, jnp.ones((n, n), BF16)], axis=1)
    return pl.pallas_call(
        _attn_kernel,
        grid=(bsz, pairs, nq // ATTN_TILES),
        in_specs=[pl.BlockSpec((ATTN_TILES * n, LANES), lambda b, p, i: (b * (nq // ATTN_TILES) + i, p)),
                  pl.BlockSpec((seq, LANES), lambda b, p, i: (b, pairs + p)),
                  pl.BlockSpec((seq, LANES), lambda b, p, i: (b, 2 * pairs + p)),
                  _resident((n, 2 * n))],
        out_specs=pl.BlockSpec((ATTN_TILES * n, LANES), lambda b, p, i: (b * (nq // ATTN_TILES) + i, p)),
        out_shape=jax.ShapeDtypeStruct((t, SB_HEADS * SB_HEAD_DIM), BF16),
        scratch_shapes=[pltpu.VMEM((nq + 1, 2 * n, LANES), BF16), pltpu.VMEM((nq + 1, 2 * n, LANES), BF16)],
        compiler_params=pltpu.CompilerParams(dimension_semantics=("arbitrary", "arbitrary", "arbitrary"),
                                             vmem_limit_bytes=VMEM_LIMIT),
        name="stick_breaking_attention",
    )(qkv, qkv, qkv, tt)


def kernel(x, ssm_norm_w, ssm_in_w, ssm_conv_w, ssm_conv_b, ssm_dt_bias, ssm_a_log, ssm_d, ssm_gate_norm_w, ssm_out_w, kv_norm_w, w_k, w_v, attn_norm_w, w_q, w_o, ffn_norm_w, ffn_up_w, ffn_conv_w, ffn_conv_b, ffn_down_w, final_norm_w):
    bsz, seq, d = x.shape
    h = x.reshape(bsz * seq, d)
    zx, dt = _in_proj(h, seq, ssm_norm_w[0], ssm_in_w[0], ssm_conv_w[0], ssm_conv_b[0], ssm_dt_bias[0])
    y = _ssd(zx, dt, bsz, seq, ssm_a_log[0], ssm_d[0], ssm_gate_norm_w[0])
    h = _proj_residual(y, ssm_out_w[0], h, seq)
    h = _ffn(h, seq, ffn_norm_w[0], ffn_up_w[0], ffn_conv_w[0], ffn_conv_b[0], ffn_down_w[0])
    qkv = _qkv_proj(h, seq, attn_norm_w[0], kv_norm_w, w_q[0], w_k, w_v)
    o = _attention(qkv, bsz, seq)
    h = _proj_residual(o, w_o[0], h, seq)
    h = _ffn(h, seq, ffn_norm_w[1], ffn_up_w[1], ffn_conv_w[1], ffn_conv_b[1], ffn_down_w[1], final_norm_w)
    return h.reshape(bsz, seq, d)
```

```python
import functools
import math

import jax
import jax.numpy as jnp
from jax import lax
from jax.experimental import pallas as pl
from jax.experimental.pallas import tpu as pltpu

D_MODEL = 1024
D_INNER = 2048
SSM_HEADS = 32
SSM_HEAD_DIM = 64
SSM_GROUPS = 4
SSM_STATE = 128
SSM_CONV = 4
SSM_CHUNK = 128
GN = SSM_GROUPS * SSM_STATE
CONV_DIM = D_INNER + 2 * GN
SB_HEADS = 16
SB_HEAD_DIM = 64
SB_BLOCK = 128
D_FF = 2816
FFN_CONV = 3
EPS = 1e-6

LANES = 128
HALO = 16
ROW_TILE = 512
COL_CHUNK = 512
FFN_CHUNK = 256
VMEM_LIMIT = 56 * 1024 * 1024
NEG_BIG = -1e30
LOG2E = math.log2(math.e)
CONV_AHEAD = 2
ATTN_GROUP = 3
ATTN_TILES = 4
ATTN_STOP = 104.0

F32 = jnp.float32
BF16 = jnp.bfloat16


def _rms(x, w):
    ms = jnp.mean(x * x, axis=-1, keepdims=True)
    return x * lax.rsqrt(ms + EPS) * w


def _softplus(x):
    return jnp.maximum(x, 0.0) + jnp.log(1.0 + jnp.exp(-jnp.abs(x)))


def _silu(x):
    return x / (1.0 + jnp.exp(-x))


def _dot(a, b):
    return jnp.dot(a, b, preferred_element_type=F32)


def _dot_nt(a, b):
    return lax.dot_general(a, b, (((1,), (1,)), ((), ())), preferred_element_type=F32)


def _split3(x):
    h1 = x.astype(BF16)
    r1 = x - h1.astype(F32)
    h2 = r1.astype(BF16)
    h3 = (r1 - h2.astype(F32)).astype(BF16)
    return h1, h2, h3


def _resident(shape):
    return pl.BlockSpec(shape, lambda *_: (0,) * len(shape), pipeline_mode=pl.Buffered(1))


def _fill_normed(xn_ref, x_ref, xh_ref, nw, first):
    xn_ref[HALO:, :] = _rms(x_ref[...], nw).astype(BF16)
    xh = _rms(xh_ref[...], nw)
    xn_ref[:HALO, :] = jnp.where(first, 0.0, xh).astype(BF16)


def _causal_conv(h_ref, hlo, cw_ref, cb_ref, lo, width, taps, rows):
    y = cb_ref[:, lo:lo + width]
    for k in range(taps):
        y = y + h_ref[pl.ds(HALO - (taps - 1) + k, rows), hlo:hlo + width] * cw_ref[k:k + 1, lo:lo + width]
    return y


def _in_proj_kernel(x_ref, xh_ref, nw_ref, w_ref, wdt_ref, cw_ref, cb_ref, dtb_ref,
                    out_ref, dt_ref, xn_ref, acc_ref, *, tm, tiles_per_seq):
    first = (pl.program_id(0) % tiles_per_seq) == 0
    _fill_normed(xn_ref, x_ref, xh_ref, nw_ref[...], first)
    xn = xn_ref[HALO:, :]
    dt_ref[...] = _softplus(_dot(xn, wdt_ref[...]) + dtb_ref[...])
    for c in range(D_INNER // COL_CHUNK):
        lo = c * COL_CHUNK
        out_ref[:, lo:lo + COL_CHUNK] = _dot(xn, w_ref[:, lo:lo + COL_CHUNK]).astype(BF16)
    nchunks = CONV_DIM // COL_CHUNK
    nbuf = acc_ref.shape[0]

    def xbc_proj(c):
        lo = D_INNER + c * COL_CHUNK
        acc_ref[c % nbuf] = _dot(xn_ref[...], w_ref[:, lo:lo + COL_CHUNK])

    for c in range(min(CONV_AHEAD, nchunks)):
        xbc_proj(c)
    for c in range(nchunks):
        lo = c * COL_CHUNK
        if c + CONV_AHEAD < nchunks:
            xbc_proj(c + CONV_AHEAD)
        y = _causal_conv(acc_ref.at[c % nbuf], 0, cw_ref, cb_ref, lo, COL_CHUNK, SSM_CONV, tm)
        out_ref[:, D_INNER + lo:D_INNER + lo + COL_CHUNK] = _silu(y).astype(BF16)


def _row_specs(tm, d):
    per = tm // HALO
    return (pl.BlockSpec((tm, d), lambda i: (i, 0)),
            pl.BlockSpec((HALO, d), lambda i: (jnp.maximum(i * per - 1, 0), 0)))


def _in_proj(h, seq, norm_w, w_in, conv_w, conv_b, dt_bias):
    t, d = h.shape
    tm = min(ROW_TILE, seq)
    w_main = w_in[:, :D_INNER + CONV_DIM].astype(BF16)
    rep = LANES // SSM_HEADS
    w_dt = jnp.tile(w_in[:, D_INNER + CONV_DIM:], (1, rep)).astype(BF16)
    dtb = jnp.tile(dt_bias, rep).reshape(1, LANES)
    x_spec, xh_spec = _row_specs(tm, d)
    return pl.pallas_call(
        functools.partial(_in_proj_kernel, tm=tm, tiles_per_seq=seq // tm),
        grid=(t // tm,),
        in_specs=[x_spec, xh_spec, _resident((1, d)), _resident(w_main.shape), _resident(w_dt.shape),
                  _resident(conv_w.shape), _resident((1, CONV_DIM)), _resident((1, LANES))],
        out_specs=[pl.BlockSpec((tm, D_INNER + CONV_DIM), lambda i: (i, 0)),
                   pl.BlockSpec((tm, LANES), lambda i: (i, 0))],
        out_shape=[jax.ShapeDtypeStruct((t, D_INNER + CONV_DIM), BF16),
                   jax.ShapeDtypeStruct((t, LANES), F32)],
        scratch_shapes=[pltpu.VMEM((tm + HALO, d), BF16), pltpu.VMEM((CONV_AHEAD + 1, tm + HALO, COL_CHUNK), F32)],
        compiler_params=pltpu.CompilerParams(dimension_semantics=("arbitrary",), vmem_limit_bytes=VMEM_LIMIT),
        name="ssm_in_proj",
    )(h, h, norm_w.reshape(1, d), w_main, w_dt, conv_w, conv_b.reshape(1, CONV_DIM), dtb)


def _ssd_kernel(z_ref, xs_ref, b_ref, c_ref, dt_ref, alog_ref, dskip_ref, gnw_ref, ee_ref, ew_ref,
                y_ref, state_ref, yg_ref, e_ref, w_ref):
    @pl.when(pl.program_id(1) == 0)
    def _():
        state_ref[...] = jnp.zeros_like(state_ref)

    n = SSM_CHUNK
    row = lax.broadcasted_iota(jnp.int32, (n, n), 0)
    col = lax.broadcasted_iota(jnp.int32, (n, n), 1)
    causal = col <= row
    low_half = col < SSM_HEAD_DIM

    dt = dt_ref[...]
    ad = dt * (-jnp.exp(alog_ref[...]) * LOG2E)
    tri = causal.astype(BF16)
    a1, a2, a3 = _split3(ad)
    acs = _dot(tri, a1) + _dot(tri, a2) + _dot(tri, a3)
    acs_last = acs[n - 1:n, :]
    src_t = (acs - jnp.log(dt) * LOG2E).T
    e_acs = jnp.exp2(acs)
    w_dec = jnp.exp2(acs_last - acs) * dt

    def hi_lo(x):
        hi = x.astype(BF16)
        return hi, (x - hi.astype(F32)).astype(BF16)

    e_hi, e_lo = hi_lo(e_acs)
    w_hi, w_lo = hi_lo(w_dec)
    q = SSM_HEADS
    packed = jnp.where(col < q, e_hi, jnp.where(col < 2 * q, e_lo, jnp.where(col < 3 * q, w_hi, w_lo)))
    for c in range(D_INNER // COL_CHUNK):
        lo = c * COL_CHUNK
        e_ref[:, lo:lo + COL_CHUNK] = _dot(packed, ee_ref[:, lo:lo + COL_CHUNK])
        w_ref[:, lo:lo + COL_CHUNK] = _dot(packed, ew_ref[:, lo:lo + COL_CHUNK])

    for g in range(SSM_GROUPS):
        gl = g * SSM_STATE
        bg = b_ref[:, gl:gl + SSM_STATE]
        cg = c_ref[:, gl:gl + SSM_STATE]
        cb = _dot_nt(cg, bg)
        bg_t = bg.astype(F32).T.astype(BF16)
        for p in range(SSM_HEADS // SSM_GROUPS // 2):
            h0 = g * (SSM_HEADS // SSM_GROUPS) + 2 * p
            lo = h0 * SSM_HEAD_DIM
            xs_pair = xs_ref[:, lo:lo + LANES]
            xs_f = xs_pair.astype(F32)
            ms = []
            for h in (h0, h0 + 1):
                diff = acs[:, h:h + 1] - src_t[h:h + 1, :]
                ms.append((cb * jnp.exp2(jnp.where(causal, diff, NEG_BIG))).astype(BF16))
            zero = jnp.zeros_like(xs_pair)
            rhs = jnp.concatenate([jnp.where(low_half, xs_pair, zero), jnp.where(low_half, zero, xs_pair)], axis=0)
            y_diag = _dot(jnp.concatenate(ms, axis=1), rhs)
            s_pair = state_ref[:, lo:lo + LANES]
            y_off = _dot(cg, s_pair.astype(BF16)) * e_ref[:, lo:lo + LANES]
            xw = (xs_f * w_ref[:, lo:lo + LANES]).astype(BF16)
            state_ref[:, lo:lo + LANES] = s_pair * e_ref[n - 1:n, lo:lo + LANES] + _dot(bg_t, xw)
            y = y_diag + y_off + dskip_ref[:, lo:lo + LANES] * xs_f
            yg_ref[:, lo:lo + LANES] = y * _silu(z_ref[:, lo:lo + LANES].astype(F32))
        gw = D_INNER // SSM_GROUPS
        yg = yg_ref[:, g * gw:(g + 1) * gw]
        y_ref[:, g * gw:(g + 1) * gw] = _rms(yg, gnw_ref[:, g * gw:(g + 1) * gw]).astype(BF16)


def _head_spread(first_group):
    k = jnp.arange(LANES)[:, None]
    head = jnp.arange(D_INNER)[None, :] // SSM_HEAD_DIM
    group = k // SSM_HEADS
    return ((k % SSM_HEADS == head) & (group >= first_group) & (group < first_group + 2)).astype(BF16)


def _ssd(zx, dt, bsz, seq, a_log, d_skip, gate_norm_w):
    t = zx.shape[0]
    nc = seq // SSM_CHUNK
    alog = jnp.tile(a_log, LANES // SSM_HEADS).reshape(1, LANES)
    dskip = jnp.repeat(d_skip, SSM_HEAD_DIM).reshape(1, D_INNER)
    rowblk = lambda b, c: b * nc + c
    return pl.pallas_call(
        _ssd_kernel,
        grid=(bsz, nc),
        in_specs=[pl.BlockSpec((SSM_CHUNK, D_INNER), lambda b, c: (rowblk(b, c), 0)),
                  pl.BlockSpec((SSM_CHUNK, D_INNER), lambda b, c: (rowblk(b, c), 1)),
                  pl.BlockSpec((SSM_CHUNK, GN), lambda b, c: (rowblk(b, c), 2 * D_INNER // GN)),
                  pl.BlockSpec((SSM_CHUNK, GN), lambda b, c: (rowblk(b, c), 2 * D_INNER // GN + 1)),
                  pl.BlockSpec((SSM_CHUNK, LANES), lambda b, c: (rowblk(b, c), 0)),
                  _resident((1, LANES)), _resident((1, D_INNER)), _resident((1, D_INNER)),
                  _resident((LANES, D_INNER)), _resident((LANES, D_INNER))],
        out_specs=pl.BlockSpec((SSM_CHUNK, D_INNER), lambda b, c: (rowblk(b, c), 0)),
        out_shape=jax.ShapeDtypeStruct((t, D_INNER), BF16),
        scratch_shapes=[pltpu.VMEM((SSM_STATE, D_INNER), F32), pltpu.VMEM((SSM_CHUNK, D_INNER), F32),
                        pltpu.VMEM((SSM_CHUNK, D_INNER), F32), pltpu.VMEM((SSM_CHUNK, D_INNER), F32)],
        compiler_params=pltpu.CompilerParams(dimension_semantics=("arbitrary", "arbitrary"),
                                             vmem_limit_bytes=VMEM_LIMIT),
        name="ssd_scan",
    )(zx, zx, zx, zx, dt, alog, dskip, gate_norm_w.reshape(1, D_INNER), _head_spread(0), _head_spread(2))


def _proj_residual_kernel(a_ref, w_ref, res_ref, out_ref):
    for c in range(D_MODEL // COL_CHUNK):
        lo = c * COL_CHUNK
        out_ref[:, lo:lo + COL_CHUNK] = res_ref[:, lo:lo + COL_CHUNK] + _dot(a_ref[...], w_ref[:, lo:lo + COL_CHUNK])


def _proj_residual(a, w, res, seq):
    t, k = a.shape
    tm = min(ROW_TILE, seq)
    return pl.pallas_call(
        _proj_residual_kernel,
        grid=(t // tm,),
        in_specs=[pl.BlockSpec((tm, k), lambda i: (i, 0)), _resident((k, D_MODEL)),
                  pl.BlockSpec((tm, D_MODEL), lambda i: (i, 0))],
        out_specs=pl.BlockSpec((tm, D_MODEL), lambda i: (i, 0)),
        out_shape=jax.ShapeDtypeStruct((t, D_MODEL), F32),
        compiler_params=pltpu.CompilerParams(dimension_semantics=("arbitrary",), vmem_limit_bytes=VMEM_LIMIT),
        name="proj_residual",
    )(a, w.astype(BF16), res)


def _ffn_kernel(x_ref, xh_ref, nw_ref, wup_ref, cw_ref, cb_ref, wd_ref, fnw_ref,
                out_ref, xn_ref, h_ref, hid_ref, *, tm, tiles_per_seq, final_norm):
    first = (pl.program_id(0) % tiles_per_seq) == 0
    _fill_normed(xn_ref, x_ref, xh_ref, nw_ref[...], first)
    fc = FFN_CHUNK
    nchunks = D_FF // fc

    nbuf = h_ref.shape[0]

    def up_proj(c):
        buf = h_ref.at[c % nbuf]
        buf[:, :fc] = _dot(xn_ref[...], wup_ref[:, c * fc:(c + 1) * fc])
        buf[:, fc:] = _dot(xn_ref[...], wup_ref[:, D_FF + c * fc:D_FF + (c + 1) * fc])

    for c in range(min(CONV_AHEAD, nchunks)):
        up_proj(c)
    for c in range(nchunks):
        if c + CONV_AHEAD < nchunks:
            up_proj(c + CONV_AHEAD)
        buf = h_ref.at[c % nbuf]
        gate = _causal_conv(buf, 0, cw_ref, cb_ref, c * fc, fc, FFN_CONV, tm)
        value = _causal_conv(buf, fc, cw_ref, cb_ref, D_FF + c * fc, fc, FFN_CONV, tm)
        hid_ref[:, c * fc:(c + 1) * fc] = (_silu(gate) * value).astype(BF16)
    for c in range(D_MODEL // COL_CHUNK):
        lo = c * COL_CHUNK
        out_ref[:, lo:lo + COL_CHUNK] = x_ref[:, lo:lo + COL_CHUNK] + _dot(hid_ref[...], wd_ref[:, lo:lo + COL_CHUNK])
    if final_norm:
        out_ref[...] = _rms(out_ref[...], fnw_ref[...])


def _ffn(h, seq, norm_w, w_up, conv_w, conv_b, w_down, final_norm_w=None):
    t, d = h.shape
    tm = min(ROW_TILE, seq)
    final_norm = final_norm_w is not None
    fnw = (final_norm_w if final_norm else jnp.ones((d,), F32)).reshape(1, d)
    x_spec, xh_spec = _row_specs(tm, d)
    return pl.pallas_call(
        functools.partial(_ffn_kernel, tm=tm, tiles_per_seq=seq // tm, final_norm=final_norm),
        grid=(t // tm,),
        in_specs=[x_spec, xh_spec, _resident((1, d)), _resident((d, 2 * D_FF)), _resident((FFN_CONV, 2 * D_FF)),
                  _resident((1, 2 * D_FF)), _resident((D_FF, d)), _resident((1, d))],
        out_specs=pl.BlockSpec((tm, d), lambda i: (i, 0)),
        out_shape=jax.ShapeDtypeStruct((t, d), F32),
        scratch_shapes=[pltpu.VMEM((tm + HALO, d), BF16), pltpu.VMEM((CONV_AHEAD + 1, tm + HALO, 2 * FFN_CHUNK), F32),
                        pltpu.VMEM((tm, D_FF), BF16)],
        compiler_params=pltpu.CompilerParams(dimension_semantics=("arbitrary",), vmem_limit_bytes=VMEM_LIMIT),
        name="conv_ffn",
    )(h, h, norm_w.reshape(1, d), w_up.astype(BF16), conv_w,
      conv_b.reshape(1, 2 * D_FF), w_down.astype(BF16), fnw)


def _qkv_kernel(x_ref, qnw_ref, kvnw_ref, w_ref, out_ref):
    x = x_ref[...]
    xhat = x * lax.rsqrt(jnp.mean(x * x, axis=-1, keepdims=True) + EPS)
    xq = (xhat * qnw_ref[...]).astype(BF16)
    xkv = (xhat * kvnw_ref[...]).astype(BF16)
    for c in range(3 * D_MODEL // COL_CHUNK):
        lo = c * COL_CHUNK
        src = xq if lo < D_MODEL else xkv
        out_ref[:, lo:lo + COL_CHUNK] = _dot(src, w_ref[:, lo:lo + COL_CHUNK]).astype(BF16)


def _qkv_proj(h, seq, q_norm_w, kv_norm_w, w_q, w_k, w_v):
    t, d = h.shape
    tm = min(ROW_TILE, seq)
    w = jnp.concatenate([w_q, w_k, w_v], axis=1).astype(BF16)
    return pl.pallas_call(
        _qkv_kernel,
        grid=(t // tm,),
        in_specs=[pl.BlockSpec((tm, d), lambda i: (i, 0)), _resident((1, d)), _resident((1, d)),
                  _resident((d, 3 * d))],
        out_specs=pl.BlockSpec((tm, 3 * d), lambda i: (i, 0)),
        out_shape=jax.ShapeDtypeStruct((t, 3 * d), BF16),
        compiler_params=pltpu.CompilerParams(dimension_semantics=("arbitrary",), vmem_limit_bytes=VMEM_LIMIT),
        name="qkv_proj",
    )(h, q_norm_w.reshape(1, d), kv_norm_w.reshape(1, d), w)


def _attn_kernel(q_ref, k_ref, v_ref, tt_ref, out_ref, ks_ref, vs_ref):
    i = pl.program_id(2)
    n = SB_BLOCK
    nblk = ks_ref.shape[0] - 1
    row = lax.broadcasted_iota(jnp.int32, (n, n), 0)
    col = lax.broadcasted_iota(jnp.int32, (n, n), 1)
    low_half = col < SB_HEAD_DIM
    before = col < row

    @pl.when(i == 0)
    def _():
        def fill(j, _):
            sl = pl.ds(pl.multiple_of(j * n, n), n)
            kj = k_ref[sl, :]
            vj = v_ref[sl, :]
            zero = jnp.zeros_like(kj)
            ks_ref[j] = jnp.concatenate([jnp.where(low_half, kj, zero), jnp.where(low_half, zero, kj)], axis=0)
            vs_ref[j] = jnp.concatenate([jnp.where(low_half, vj, zero), jnp.where(low_half, zero, vj)], axis=0)
            return 0
        lax.fori_loop(0, nblk, fill, 0)
        ks_ref[nblk] = jnp.zeros(ks_ref.shape[1:], BF16)
        vs_ref[nblk] = jnp.zeros(vs_ref.shape[1:], BF16)

    tt = tt_ref[...]
    scale = 1.0 / math.sqrt(SB_HEAD_DIM)
    tiles = [i * ATTN_TILES + u for u in range(ATTN_TILES)]
    qss = [q_ref[u * n:(u + 1) * n, :] * scale for u in range(ATTN_TILES)]

    def sweep(us, ts, states, diagonal):
        work = [(u, g) for u in range(len(us)) for g in range(ATTN_GROUP)]
        jcs, z2s, sp2s, laters = {}, {}, {}, {}
        for u, g in work:
            j = tiles[us[u]] - (ts[u] * ATTN_GROUP + g)
            jcs[u, g] = jnp.where(j >= 0, j, nblk)
            z2s[u, g] = _dot_nt(qss[us[u]], ks_ref[jcs[u, g]])
        for key in work:
            sp2s[key] = _softplus(z2s[key])
        carries = [list(st[0]) for st in states]
        for u, g in work:
            for h in range(2):
                sp = sp2s[u, g][:, h * n:(h + 1) * n]
                if diagonal and g == 0:
                    sp = jnp.where(before, sp, 0.0)
                cs = _dot(sp.astype(BF16), tt)
                if carries[u][h] is None:
                    laters[u, g, h], carries[u][h] = cs[:, :n], cs[:, n:]
                else:
                    laters[u, g, h] = cs[:, :n] + carries[u][h]
                    carries[u][h] = carries[u][h] + cs[:, n:]
        mins = [jnp.min(jnp.minimum(c[0], c[1])) for c in carries]
        accs = [st[1] for st in states]
        for u, g in work:
            atts = []
            for h in range(2):
                sl = slice(h * n, (h + 1) * n)
                att = jnp.exp(z2s[u, g][:, sl] - sp2s[u, g][:, sl] - laters[u, g, h])
                if diagonal and g == 0:
                    att = jnp.where(before, att, 0.0)
                atts.append(att.astype(BF16))
            pv = _dot(jnp.concatenate(atts, axis=1), vs_ref[jcs[u, g]])
            accs[u] = pv if accs[u] is None else accs[u] + pv
        return [(carries[u], accs[u], mins[u]) for u in range(len(us))]

    us = list(range(ATTN_TILES))
    first = sweep(us, [0] * ATTN_TILES, [([None, None], None)] * ATTN_TILES, True)
    for u in us:
        carries, acc, min_carry = first[u]
        ngroups = (tiles[u] + ATTN_GROUP) // ATTN_GROUP

        def more(state, ngroups=ngroups):
            return jnp.logical_and(state[0] < ngroups, state[1] <= ATTN_STOP)

        def group(state, u=u):
            t, _, c0, c1, acc = state
            (carries, acc, min_carry), = sweep([u], [t], [([c0, c1], acc)], False)
            return t + 1, min_carry, carries[0], carries[1], acc

        state = lax.while_loop(more, group, (jnp.int32(1), min_carry, carries[0], carries[1], acc))
        out_ref[u * n:(u + 1) * n, :] = state[4].astype(BF16)


def _attention(qkv, bsz, seq):
    t = qkv.shape[0]
    n = SB_BLOCK
    nq = seq // n
    pairs = SB_HEADS // 2
    r = jnp.arange(n)
    tri = (r[:, None] > r[None, :]).astype(BF16)
    tt = jnp.concatenate([tri, jnp.ones((n, n), BF16)], axis=1)
    return pl.pallas_call(
        _attn_kernel,
        grid=(bsz, pairs, nq // ATTN_TILES),
        in_specs=[pl.BlockSpec((ATTN_TILES * n, LANES), lambda b, p, i: (b * (nq // ATTN_TILES) + i, p)),
                  pl.BlockSpec((seq, LANES), lambda b, p, i: (b, pairs + p)),
                  pl.BlockSpec((seq, LANES), lambda b, p, i: (b, 2 * pairs + p)),
                  _resident((n, 2 * n))],
        out_specs=pl.BlockSpec((ATTN_TILES * n, LANES), lambda b, p, i: (b * (nq // ATTN_TILES) + i, p)),
        out_shape=jax.ShapeDtypeStruct((t, SB_HEADS * SB_HEAD_DIM), BF16),
        scratch_shapes=[pltpu.VMEM((nq + 1, 2 * n, LANES), BF16), pltpu.VMEM((nq + 1, 2 * n, LANES), BF16)],
        compiler_params=pltpu.CompilerParams(dimension_semantics=("arbitrary", "arbitrary", "arbitrary"),
                                             vmem_limit_bytes=VMEM_LIMIT),
        name="stick_breaking_attention",
    )(qkv, qkv, qkv, tt)


def kernel(x, ssm_norm_w, ssm_in_w, ssm_conv_w, ssm_conv_b, ssm_dt_bias, ssm_a_log, ssm_d, ssm_gate_norm_w, ssm_out_w, kv_norm_w, w_k, w_v, attn_norm_w, w_q, w_o, ffn_norm_w, ffn_up_w, ffn_conv_w, ffn_conv_b, ffn_down_w, final_norm_w):
    bsz, seq, d = x.shape
    h = x.reshape(bsz * seq, d)
    zx, dt = _in_proj(h, seq, ssm_norm_w[0], ssm_in_w[0], ssm_conv_w[0], ssm_conv_b[0], ssm_dt_bias[0])
    y = _ssd(zx, dt, bsz, seq, ssm_a_log[0], ssm_d[0], ssm_gate_norm_w[0])
    h = _proj_residual(y, ssm_out_w[0], h, seq)
    h = _ffn(h, seq, ffn_norm_w[0], ffn_up_w[0], ffn_conv_w[0], ffn_conv_b[0], ffn_down_w[0])
    qkv = _qkv_proj(h, seq, attn_norm_w[0], kv_norm_w, w_q[0], w_k, w_v)
    o = _attention(qkv, bsz, seq)
    h = _proj_residual(o, w_o[0], h, seq)
    h = _ffn(h, seq, ffn_norm_w[1], ffn_up_w[1], ffn_conv_w[1], ffn_conv_b[1], ffn_down_w[1], final_norm_w)
    return h.reshape(bsz, seq, d)
```

```python
import functools
import math

import jax
import jax.numpy as jnp
from jax import lax
from jax.experimental import pallas as pl
from jax.experimental.pallas import tpu as pltpu

D_MODEL = 1024
D_INNER = 2048
SSM_HEADS = 32
SSM_HEAD_DIM = 64
SSM_GROUPS = 4
SSM_STATE = 128
SSM_CONV = 4
SSM_CHUNK = 128
GN = SSM_GROUPS * SSM_STATE
CONV_DIM = D_INNER + 2 * GN
SB_HEADS = 16
SB_HEAD_DIM = 64
SB_BLOCK = 128
D_FF = 2816
FFN_CONV = 3
EPS = 1e-6

LANES = 128
HALO = 16
ROW_TILE = 512
COL_CHUNK = 512
FFN_CHUNK = 256
VMEM_LIMIT = 56 * 1024 * 1024
NEG_BIG = -1e30
LOG2E = math.log2(math.e)
CONV_ROWS = 128
CONV_AHEAD = 2
SSD_SUB = 2
ATTN_GROUP = 3
ATTN_TILES = 8
ATTN_STOP = 104.0

F32 = jnp.float32
BF16 = jnp.bfloat16


def _rms(x, w):
    ms = jnp.mean(x * x, axis=-1, keepdims=True)
    return x * lax.rsqrt(ms + EPS) * w


def _softplus(x):
    return jnp.maximum(x, 0.0) + jnp.log(1.0 + jnp.exp(-jnp.abs(x)))


def _silu(x):
    return x / (1.0 + jnp.exp(-x))


def _dot(a, b):
    return jnp.dot(a, b, preferred_element_type=F32)


def _dot_nt(a, b):
    return lax.dot_general(a, b, (((1,), (1,)), ((), ())), preferred_element_type=F32)


def _split3(x):
    h1 = x.astype(BF16)
    r1 = x - h1.astype(F32)
    h2 = r1.astype(BF16)
    h3 = (r1 - h2.astype(F32)).astype(BF16)
    return h1, h2, h3


def _resident(shape):
    return pl.BlockSpec(shape, lambda *_: (0,) * len(shape), pipeline_mode=pl.Buffered(1))


def _fill_normed(xn_ref, x_ref, xh_ref, nw, first):
    xn_ref[HALO:, :] = _rms(x_ref[...], nw).astype(BF16)
    xh = _rms(xh_ref[...], nw)
    xn_ref[:HALO, :] = jnp.where(first, 0.0, xh).astype(BF16)


def _store_panels(buf, first_panel, res):
    for p in range(res.shape[1] // LANES):
        buf[first_panel + p] = res[:, p * LANES:(p + 1) * LANES]


def _causal_conv(buf, panel, cw_ref, cb_ref, ch, taps, r0, rows):
    y = cb_ref[:, ch:ch + LANES]
    for k in range(taps):
        y = y + buf[panel, pl.ds(HALO - (taps - 1) + k + r0, rows), :] * cw_ref[k:k + 1, ch:ch + LANES]
    return y


def _in_proj_kernel(x_ref, xh_ref, nw_ref, w_ref, wdt_ref, cw_ref, cb_ref, dtb_ref,
                    out_ref, dt_ref, xn_ref, acc_ref, *, tm, tiles_per_seq):
    first = (pl.program_id(0) % tiles_per_seq) == 0
    _fill_normed(xn_ref, x_ref, xh_ref, nw_ref[...], first)
    xn = xn_ref[HALO:, :]
    dt_ref[...] = _softplus(_dot(xn, wdt_ref[...]) + dtb_ref[...])
    for c in range(D_INNER // COL_CHUNK):
        lo = c * COL_CHUNK
        out_ref[:, lo:lo + COL_CHUNK] = _dot(xn, w_ref[:, lo:lo + COL_CHUNK]).astype(BF16)
    nchunks = CONV_DIM // COL_CHUNK
    nbuf = acc_ref.shape[0]

    def xbc_proj(c):
        lo = D_INNER + c * COL_CHUNK
        _store_panels(acc_ref.at[c % nbuf], 0, _dot(xn_ref[...], w_ref[:, lo:lo + COL_CHUNK]))

    for c in range(min(CONV_AHEAD, nchunks)):
        xbc_proj(c)
    for c in range(nchunks):
        lo = c * COL_CHUNK
        if c + CONV_AHEAD < nchunks:
            xbc_proj(c + CONV_AHEAD)
        for p in range(COL_CHUNK // LANES):
            ch = lo + p * LANES
            for r0 in range(0, tm, CONV_ROWS):
                y = _causal_conv(acc_ref.at[c % nbuf], p, cw_ref, cb_ref, ch, SSM_CONV, r0, CONV_ROWS)
                out_ref[r0:r0 + CONV_ROWS, D_INNER + ch:D_INNER + ch + LANES] = _silu(y).astype(BF16)


def _row_specs(tm, d):
    per = tm // HALO
    return (pl.BlockSpec((tm, d), lambda i: (i, 0)),
            pl.BlockSpec((HALO, d), lambda i: (jnp.maximum(i * per - 1, 0), 0)))


def _in_proj(h, seq, norm_w, w_in, conv_w, conv_b, dt_bias):
    t, d = h.shape
    tm = min(ROW_TILE, seq)
    w_main = w_in[:, :D_INNER + CONV_DIM].astype(BF16)
    rep = LANES // SSM_HEADS
    w_dt = jnp.tile(w_in[:, D_INNER + CONV_DIM:], (1, rep)).astype(BF16)
    dtb = jnp.tile(dt_bias, rep).reshape(1, LANES)
    x_spec, xh_spec = _row_specs(tm, d)
    return pl.pallas_call(
        functools.partial(_in_proj_kernel, tm=tm, tiles_per_seq=seq // tm),
        grid=(t // tm,),
        in_specs=[x_spec, xh_spec, _resident((1, d)), _resident(w_main.shape), _resident(w_dt.shape),
                  _resident(conv_w.shape), _resident((1, CONV_DIM)), _resident((1, LANES))],
        out_specs=[pl.BlockSpec((tm, D_INNER + CONV_DIM), lambda i: (i, 0)),
                   pl.BlockSpec((tm, LANES), lambda i: (i, 0))],
        out_shape=[jax.ShapeDtypeStruct((t, D_INNER + CONV_DIM), BF16),
                   jax.ShapeDtypeStruct((t, LANES), F32)],
        scratch_shapes=[pltpu.VMEM((tm + HALO, d), BF16), pltpu.VMEM((CONV_AHEAD + 1, COL_CHUNK // LANES, tm + HALO, LANES), F32)],
        compiler_params=pltpu.CompilerParams(dimension_semantics=("arbitrary",), vmem_limit_bytes=VMEM_LIMIT),
        name="ssm_in_proj",
    )(h, h, norm_w.reshape(1, d), w_main, w_dt, conv_w, conv_b.reshape(1, CONV_DIM), dtb)


def _ssd_kernel(z_ref, xs_ref, b_ref, c_ref, dt_ref, alog_ref, dskip_ref, gnw_ref, ee_ref, ew_ref,
                y_ref, state_ref, yg_ref, e_ref, w_ref):
    @pl.when(pl.program_id(1) == 0)
    def _():
        state_ref[...] = jnp.zeros_like(state_ref)

    n = SSM_CHUNK
    row = lax.broadcasted_iota(jnp.int32, (n, n), 0)
    col = lax.broadcasted_iota(jnp.int32, (n, n), 1)
    causal = col <= row
    low_half = col < SSM_HEAD_DIM
    tri = causal.astype(BF16)

    def hi_lo(x):
        hi = x.astype(BF16)
        return hi, (x - hi.astype(F32)).astype(BF16)

    def decays(u):
        dt = dt_ref[u * n:(u + 1) * n, :]
        ad = dt * (-jnp.exp(alog_ref[...]) * LOG2E)
        a1, a2, a3 = _split3(ad)
        acs = _dot(tri, a1) + _dot(tri, a2) + _dot(tri, a3)
        acs_last = acs[n - 1:n, :]
        src_t = (acs - jnp.log(dt) * LOG2E).T
        e_hi, e_lo = hi_lo(jnp.exp2(acs))
        w_hi, w_lo = hi_lo(jnp.exp2(acs_last - acs) * dt)
        q = SSM_HEADS
        packed = jnp.where(col < q, e_hi, jnp.where(col < 2 * q, e_lo, jnp.where(col < 3 * q, w_hi, w_lo)))
        for c in range(D_INNER // COL_CHUNK):
            lo = c * COL_CHUNK
            e_ref[u, :, lo:lo + COL_CHUNK] = _dot(packed, ee_ref[:, lo:lo + COL_CHUNK])
            w_ref[u, :, lo:lo + COL_CHUNK] = _dot(packed, ew_ref[:, lo:lo + COL_CHUNK])
        return acs, src_t

    terms = [decays(u) for u in range(SSD_SUB)]
    for u in range(SSD_SUB):
        acs, src_t = terms[u]
        rows = slice(u * n, (u + 1) * n)
        for g in range(SSM_GROUPS):
            gl = g * SSM_STATE
            bg = b_ref[rows, gl:gl + SSM_STATE]
            cg = c_ref[rows, gl:gl + SSM_STATE]
            cb = _dot_nt(cg, bg)
            bg_t = bg.astype(F32).T.astype(BF16)
            for p in range(SSM_HEADS // SSM_GROUPS // 2):
                h0 = g * (SSM_HEADS // SSM_GROUPS) + 2 * p
                lo = h0 * SSM_HEAD_DIM
                xs_pair = xs_ref[rows, lo:lo + LANES]
                xs_f = xs_pair.astype(F32)
                ms = []
                for h in (h0, h0 + 1):
                    diff = acs[:, h:h + 1] - src_t[h:h + 1, :]
                    ms.append((cb * jnp.exp2(jnp.where(causal, diff, NEG_BIG))).astype(BF16))
                zero = jnp.zeros_like(xs_pair)
                rhs = jnp.concatenate([jnp.where(low_half, xs_pair, zero), jnp.where(low_half, zero, xs_pair)], axis=0)
                y_diag = _dot(jnp.concatenate(ms, axis=1), rhs)
                s_pair = state_ref[:, lo:lo + LANES]
                y_off = _dot(cg, s_pair.astype(BF16)) * e_ref[u, :, lo:lo + LANES]
                xw = (xs_f * w_ref[u, :, lo:lo + LANES]).astype(BF16)
                state_ref[:, lo:lo + LANES] = s_pair * e_ref[u, n - 1:n, lo:lo + LANES] + _dot(bg_t, xw)
                y = y_diag + y_off + dskip_ref[:, lo:lo + LANES] * xs_f
                yg_ref[u, :, lo:lo + LANES] = y * _silu(z_ref[rows, lo:lo + LANES].astype(F32))
            gw = D_INNER // SSM_GROUPS
            yg = yg_ref[u, :, g * gw:(g + 1) * gw]
            y_ref[rows, g * gw:(g + 1) * gw] = _rms(yg, gnw_ref[:, g * gw:(g + 1) * gw]).astype(BF16)


def _head_spread(first_group):
    k = jnp.arange(LANES)[:, None]
    head = jnp.arange(D_INNER)[None, :] // SSM_HEAD_DIM
    group = k // SSM_HEADS
    return ((k % SSM_HEADS == head) & (group >= first_group) & (group < first_group + 2)).astype(BF16)


def _ssd(zx, dt, bsz, seq, a_log, d_skip, gate_norm_w):
    t = zx.shape[0]
    rows = SSD_SUB * SSM_CHUNK
    nc = seq // rows
    alog = jnp.tile(a_log, LANES // SSM_HEADS).reshape(1, LANES)
    dskip = jnp.repeat(d_skip, SSM_HEAD_DIM).reshape(1, D_INNER)
    rowblk = lambda b, c: b * nc + c
    return pl.pallas_call(
        _ssd_kernel,
        grid=(bsz, nc),
        in_specs=[pl.BlockSpec((rows, D_INNER), lambda b, c: (rowblk(b, c), 0)),
                  pl.BlockSpec((rows, D_INNER), lambda b, c: (rowblk(b, c), 1)),
                  pl.BlockSpec((rows, GN), lambda b, c: (rowblk(b, c), 2 * D_INNER // GN)),
                  pl.BlockSpec((rows, GN), lambda b, c: (rowblk(b, c), 2 * D_INNER // GN + 1)),
                  pl.BlockSpec((rows, LANES), lambda b, c: (rowblk(b, c), 0)),
                  _resident((1, LANES)), _resident((1, D_INNER)), _resident((1, D_INNER)),
                  _resident((LANES, D_INNER)), _resident((LANES, D_INNER))],
        out_specs=pl.BlockSpec((rows, D_INNER), lambda b, c: (rowblk(b, c), 0)),
        out_shape=jax.ShapeDtypeStruct((t, D_INNER), BF16),
        scratch_shapes=[pltpu.VMEM((SSM_STATE, D_INNER), F32)] + [pltpu.VMEM((SSD_SUB, SSM_CHUNK, D_INNER), F32)] * 3,
        compiler_params=pltpu.CompilerParams(dimension_semantics=("arbitrary", "arbitrary"),
                                             vmem_limit_bytes=VMEM_LIMIT),
        name="ssd_scan",
    )(zx, zx, zx, zx, dt, alog, dskip, gate_norm_w.reshape(1, D_INNER), _head_spread(0), _head_spread(2))


def _proj_residual_kernel(a_ref, w_ref, res_ref, out_ref):
    for c in range(D_MODEL // COL_CHUNK):
        lo = c * COL_CHUNK
        out_ref[:, lo:lo + COL_CHUNK] = res_ref[:, lo:lo + COL_CHUNK] + _dot(a_ref[...], w_ref[:, lo:lo + COL_CHUNK])


def _proj_residual(a, w, res, seq):
    t, k = a.shape
    tm = min(ROW_TILE, seq)
    return pl.pallas_call(
        _proj_residual_kernel,
        grid=(t // tm,),
        in_specs=[pl.BlockSpec((tm, k), lambda i: (i, 0)), _resident((k, D_MODEL)),
                  pl.BlockSpec((tm, D_MODEL), lambda i: (i, 0))],
        out_specs=pl.BlockSpec((tm, D_MODEL), lambda i: (i, 0)),
        out_shape=jax.ShapeDtypeStruct((t, D_MODEL), F32),
        compiler_params=pltpu.CompilerParams(dimension_semantics=("arbitrary",), vmem_limit_bytes=VMEM_LIMIT),
        name="proj_residual",
    )(a, w.astype(BF16), res)


def _ffn_kernel(x_ref, xh_ref, nw_ref, wup_ref, cw_ref, cb_ref, wd_ref, fnw_ref,
                out_ref, xn_ref, h_ref, hid_ref, *, tm, tiles_per_seq, final_norm):
    first = (pl.program_id(0) % tiles_per_seq) == 0
    _fill_normed(xn_ref, x_ref, xh_ref, nw_ref[...], first)
    fc = FFN_CHUNK
    nchunks = D_FF // fc

    nbuf = h_ref.shape[0]

    def up_proj(c):
        buf = h_ref.at[c % nbuf]
        _store_panels(buf, 0, _dot(xn_ref[...], wup_ref[:, c * fc:(c + 1) * fc]))
        _store_panels(buf, fc // LANES, _dot(xn_ref[...], wup_ref[:, D_FF + c * fc:D_FF + (c + 1) * fc]))

    for c in range(min(CONV_AHEAD, nchunks)):
        up_proj(c)
    for c in range(nchunks):
        if c + CONV_AHEAD < nchunks:
            up_proj(c + CONV_AHEAD)
        buf = h_ref.at[c % nbuf]
        for p in range(fc // LANES):
            ch = c * fc + p * LANES
            for r0 in range(0, tm, CONV_ROWS):
                gate = _causal_conv(buf, p, cw_ref, cb_ref, ch, FFN_CONV, r0, CONV_ROWS)
                value = _causal_conv(buf, fc // LANES + p, cw_ref, cb_ref, D_FF + ch, FFN_CONV, r0, CONV_ROWS)
                hid_ref[r0:r0 + CONV_ROWS, ch:ch + LANES] = (_silu(gate) * value).astype(BF16)
    for c in range(D_MODEL // COL_CHUNK):
        lo = c * COL_CHUNK
        out_ref[:, lo:lo + COL_CHUNK] = x_ref[:, lo:lo + COL_CHUNK] + _dot(hid_ref[...], wd_ref[:, lo:lo + COL_CHUNK])
    if final_norm:
        out_ref[...] = _rms(out_ref[...], fnw_ref[...])


def _ffn(h, seq, norm_w, w_up, conv_w, conv_b, w_down, final_norm_w=None):
    t, d = h.shape
    tm = min(ROW_TILE, seq)
    final_norm = final_norm_w is not None
    fnw = (final_norm_w if final_norm else jnp.ones((d,), F32)).reshape(1, d)
    x_spec, xh_spec = _row_specs(tm, d)
    return pl.pallas_call(
        functools.partial(_ffn_kernel, tm=tm, tiles_per_seq=seq // tm, final_norm=final_norm),
        grid=(t // tm,),
        in_specs=[x_spec, xh_spec, _resident((1, d)), _resident((d, 2 * D_FF)), _resident((FFN_CONV, 2 * D_FF)),
                  _resident((1, 2 * D_FF)), _resident((D_FF, d)), _resident((1, d))],
        out_specs=pl.BlockSpec((tm, d), lambda i: (i, 0)),
        out_shape=jax.ShapeDtypeStruct((t, d), F32),
        scratch_shapes=[pltpu.VMEM((tm + HALO, d), BF16), pltpu.VMEM((CONV_AHEAD + 1, 2 * FFN_CHUNK // LANES, tm + HALO, LANES), F32),
                        pltpu.VMEM((tm, D_FF), BF16)],
        compiler_params=pltpu.CompilerParams(dimension_semantics=("arbitrary",), vmem_limit_bytes=VMEM_LIMIT),
        name="conv_ffn",
    )(h, h, norm_w.reshape(1, d), w_up.astype(BF16), conv_w,
      conv_b.reshape(1, 2 * D_FF), w_down.astype(BF16), fnw)


def _qkv_kernel(x_ref, qnw_ref, kvnw_ref, w_ref, out_ref):
    x = x_ref[...]
    xhat = x * lax.rsqrt(jnp.mean(x * x, axis=-1, keepdims=True) + EPS)
    xq = (xhat * qnw_ref[...]).astype(BF16)
    xkv = (xhat * kvnw_ref[...]).astype(BF16)
    for c in range(3 * D_MODEL // COL_CHUNK):
        lo = c * COL_CHUNK
        src = xq if lo < D_MODEL else xkv
        out_ref[:, lo:lo + COL_CHUNK] = _dot(src, w_ref[:, lo:lo + COL_CHUNK]).astype(BF16)


def _qkv_proj(h, seq, q_norm_w, kv_norm_w, w_q, w_k, w_v):
    t, d = h.shape
    tm = min(ROW_TILE, seq)
    w = jnp.concatenate([w_q, w_k, w_v], axis=1).astype(BF16)
    return pl.pallas_call(
        _qkv_kernel,
        grid=(t // tm,),
        in_specs=[pl.BlockSpec((tm, d), lambda i: (i, 0)), _resident((1, d)), _resident((1, d)),
                  _resident((d, 3 * d))],
        out_specs=pl.BlockSpec((tm, 3 * d), lambda i: (i, 0)),
        out_shape=jax.ShapeDtypeStruct((t, 3 * d), BF16),
        compiler_params=pltpu.CompilerParams(dimension_semantics=("arbitrary",), vmem_limit_bytes=VMEM_LIMIT),
        name="qkv_proj",
    )(h, q_norm_w.reshape(1, d), kv_norm_w.reshape(1, d), w)


def _attn_kernel(q_ref, k_ref, v_ref, tt_ref, out_ref, ks_ref, vs_ref):
    i = pl.program_id(2)
    n = SB_BLOCK
    nblk = ks_ref.shape[0] - 1
    row = lax.broadcasted_iota(jnp.int32, (n, n), 0)
    col = lax.broadcasted_iota(jnp.int32, (n, n), 1)
    low_half = col < SB_HEAD_DIM
    before = col < row

    @pl.when(i == 0)
    def _():
        def fill(j, _):
            sl = pl.ds(pl.multiple_of(j * n, n), n)
            kj = k_ref[sl, :]
            vj = v_ref[sl, :]
            zero = jnp.zeros_like(kj)
            ks_ref[j] = jnp.concatenate([jnp.where(low_half, kj, zero), jnp.where(low_half, zero, kj)], axis=0)
            vs_ref[j] = jnp.concatenate([jnp.where(low_half, vj, zero), jnp.where(low_half, zero, vj)], axis=0)
            return 0
        lax.fori_loop(0, nblk, fill, 0)
        ks_ref[nblk] = jnp.zeros(ks_ref.shape[1:], BF16)
        vs_ref[nblk] = jnp.zeros(vs_ref.shape[1:], BF16)

    tt = tt_ref[...]
    scale = 1.0 / math.sqrt(SB_HEAD_DIM)
    tiles = [i * ATTN_TILES + u for u in range(ATTN_TILES)]
    qss = [q_ref[u * n:(u + 1) * n, :] * scale for u in range(ATTN_TILES)]

    def sweep(us, ts, states, diagonal):
        work = [(u, g) for u in range(len(us)) for g in range(ATTN_GROUP)]
        jcs, z2s, sp2s, laters = {}, {}, {}, {}
        for u, g in work:
            j = tiles[us[u]] - (ts[u] * ATTN_GROUP + g)
            jcs[u, g] = jnp.where(j >= 0, j, nblk)
            z2s[u, g] = _dot_nt(qss[us[u]], ks_ref[jcs[u, g]])
        for key in work:
            sp2s[key] = _softplus(z2s[key])
        carries = [list(st[0]) for st in states]
        for u, g in work:
            for h in range(2):
                sp = sp2s[u, g][:, h * n:(h + 1) * n]
                if diagonal and g == 0:
                    sp = jnp.where(before, sp, 0.0)
                cs = _dot(sp.astype(BF16), tt)
                if carries[u][h] is None:
                    laters[u, g, h], carries[u][h] = cs[:, :n], cs[:, n:]
                else:
                    laters[u, g, h] = cs[:, :n] + carries[u][h]
                    carries[u][h] = carries[u][h] + cs[:, n:]
        mins = [jnp.min(jnp.minimum(c[0], c[1])) for c in carries]
        accs = [st[1] for st in states]
        for u, g in work:
            atts = []
            for h in range(2):
                sl = slice(h * n, (h + 1) * n)
                att = jnp.exp(z2s[u, g][:, sl] - sp2s[u, g][:, sl] - laters[u, g, h])
                if diagonal and g == 0:
                    att = jnp.where(before, att, 0.0)
                atts.append(att.astype(BF16))
            pv = _dot(jnp.concatenate(atts, axis=1), vs_ref[jcs[u, g]])
            accs[u] = pv if accs[u] is None else accs[u] + pv
        return [(carries[u], accs[u], mins[u]) for u in range(len(us))]

    us = list(range(ATTN_TILES))
    first = sweep(us, [0] * ATTN_TILES, [([None, None], None)] * ATTN_TILES, True)
    for u in us:
        carries, acc, min_carry = first[u]
        ngroups = (tiles[u] + ATTN_GROUP) // ATTN_GROUP

        def more(state, ngroups=ngroups):
            return jnp.logical_and(state[0] < ngroups, state[1] <= ATTN_STOP)

        def group(state, u=u):
            t, _, c0, c1, acc = state
            (carries, acc, min_carry), = sweep([u], [t], [([c0, c1], acc)], False)
            return t + 1, min_carry, carries[0], carries[1], acc

        state = lax.while_loop(more, group, (jnp.int32(1), min_carry, carries[0], carries[1], acc))
        out_ref[u * n:(u + 1) * n, :] = state[4].astype(BF16)


def _attention(qkv, bsz, seq):
    t = qkv.shape[0]
    n = SB_BLOCK
    nq = seq // n
    pairs = SB_HEADS // 2
    r = jnp.arange(n)
    tri = (r[:, None] > r[None, :]).astype(BF16)
    tt = jnp.concatenate([tri, jnp.ones((n, n), BF16)], axis=1)
    return pl.pallas_call(
        _attn_kernel,
        grid=(bsz, pairs, nq // ATTN_TILES),
        in_specs=[pl.BlockSpec((ATTN_TILES * n, LANES), lambda b, p, i: (b * (nq // ATTN_TILES) + i, p)),
                  pl.BlockSpec((seq, LANES), lambda b, p, i: (b, pairs + p)),
                  pl.BlockSpec((seq, LANES), lambda b, p, i: (b, 2 * pairs + p)),
                  _resident((n, 2 * n))],
        out_specs=pl.BlockSpec((ATTN_TILES * n, LANES), lambda b, p, i: (b * (nq // ATTN_TILES) + i, p)),
        out_shape=jax.ShapeDtypeStruct((t, SB_HEADS * SB_HEAD_DIM), BF16),
        scratch_shapes=[pltpu.VMEM((nq + 1, 2 * n, LANES), BF16), pltpu.VMEM((nq + 1, 2 * n, LANES), BF16)],
        compiler_params=pltpu.CompilerParams(dimension_semantics=("arbitrary", "arbitrary", "arbitrary"),
                                             vmem_limit_bytes=VMEM_LIMIT),
        name="stick_breaking_attention",
    )(qkv, qkv, qkv, tt)


def kernel(x, ssm_norm_w, ssm_in_w, ssm_conv_w, ssm_conv_b, ssm_dt_bias, ssm_a_log, ssm_d, ssm_gate_norm_w, ssm_out_w, kv_norm_w, w_k, w_v, attn_norm_w, w_q, w_o, ffn_norm_w, ffn_up_w, ffn_conv_w, ffn_conv_b, ffn_down_w, final_norm_w):
    bsz, seq, d = x.shape
    h = x.reshape(bsz * seq, d)
    zx, dt = _in_proj(h, seq, ssm_norm_w[0], ssm_in_w[0], ssm_conv_w[0], ssm_conv_b[0], ssm_dt_bias[0])
    y = _ssd(zx, dt, bsz, seq, ssm_a_log[0], ssm_d[0], ssm_gate_norm_w[0])
    h = _proj_residual(y, ssm_out_w[0], h, seq)
    h = _ffn(h, seq, ffn_norm_w[0], ffn_up_w[0], ffn_conv_w[0], ffn_conv_b[0], ffn_down_w[0])
    qkv = _qkv_proj(h, seq, attn_norm_w[0], kv_norm_w, w_q[0], w_k, w_v)
    o = _attention(qkv, bsz, seq)
    h = _proj_residual(o, w_o[0], h, seq)
    h = _ffn(h, seq, ffn_norm_w[1], ffn_up_w[1], ffn_conv_w[1], ffn_conv_b[1], ffn_down_w[1], final_norm_w)
    return h.reshape(bsz, seq, d)
```

```python
import functools
import math

import jax
import jax.numpy as jnp
from jax import lax
from jax.experimental import pallas as pl
from jax.experimental.pallas import tpu as pltpu

D_MODEL = 1024
D_INNER = 2048
SSM_HEADS = 32
SSM_HEAD_DIM = 64
SSM_GROUPS = 4
SSM_STATE = 128
SSM_CONV = 4
SSM_CHUNK = 128
GN = SSM_GROUPS * SSM_STATE
CONV_DIM = D_INNER + 2 * GN
SB_HEADS = 16
SB_HEAD_DIM = 64
SB_BLOCK = 128
D_FF = 2816
FFN_CONV = 3
EPS = 1e-6

LANES = 128
HALO = 16
ROW_TILE = 512
COL_CHUNK = 512
FFN_CHUNK = 256
VMEM_LIMIT = 56 * 1024 * 1024
NEG_BIG = -1e30
LOG2E = math.log2(math.e)
CONV_ROWS = 128
CONV_AHEAD = 2
SSD_SUB = 2
ATTN_GROUP = 3
ATTN_TILES = 8
Q_SCALE = LOG2E / math.sqrt(SB_HEAD_DIM)
ATTN_STOP = 104.0 * LOG2E

F32 = jnp.float32
BF16 = jnp.bfloat16


def _rms(x, w):
    ms = jnp.mean(x * x, axis=-1, keepdims=True)
    return x * lax.rsqrt(ms + EPS) * w


def _softplus(x):
    return jnp.maximum(x, 0.0) + jnp.log(1.0 + jnp.exp(-jnp.abs(x)))


def _softplus_log2(x):
    return jnp.maximum(x, 0.0) + jnp.log(1.0 + jnp.exp2(-jnp.abs(x))) * LOG2E


def _silu(x):
    return x / (1.0 + jnp.exp(-x))


def _dot(a, b):
    return jnp.dot(a, b, preferred_element_type=F32)


def _dot_nt(a, b):
    return lax.dot_general(a, b, (((1,), (1,)), ((), ())), preferred_element_type=F32)


def _split3(x):
    h1 = x.astype(BF16)
    r1 = x - h1.astype(F32)
    h2 = r1.astype(BF16)
    h3 = (r1 - h2.astype(F32)).astype(BF16)
    return h1, h2, h3


def _resident(shape):
    return pl.BlockSpec(shape, lambda *_: (0,) * len(shape), pipeline_mode=pl.Buffered(1))


def _fill_normed(xn_ref, x_ref, xh_ref, nw, first):
    xn_ref[HALO:, :] = _rms(x_ref[...], nw).astype(BF16)
    xh = _rms(xh_ref[...], nw)
    xn_ref[:HALO, :] = jnp.where(first, 0.0, xh).astype(BF16)


def _store_panels(buf, first_panel, res):
    for p in range(res.shape[1] // LANES):
        buf[first_panel + p] = res[:, p * LANES:(p + 1) * LANES]


def _causal_conv(buf, panel, cw_ref, cb_ref, ch, taps, r0, rows):
    y = cb_ref[:, ch:ch + LANES]
    for k in range(taps):
        y = y + buf[panel, pl.ds(HALO - (taps - 1) + k + r0, rows), :] * cw_ref[k:k + 1, ch:ch + LANES]
    return y


def _in_proj_kernel(x_ref, xh_ref, nw_ref, w_ref, wdt_ref, cw_ref, cb_ref, dtb_ref,
                    out_ref, dt_ref, xn_ref, acc_ref, *, tm, tiles_per_seq):
    first = (pl.program_id(0) % tiles_per_seq) == 0
    _fill_normed(xn_ref, x_ref, xh_ref, nw_ref[...], first)
    xn = xn_ref[HALO:, :]
    dt_ref[...] = _softplus(_dot(xn, wdt_ref[...]) + dtb_ref[...])
    for c in range(D_INNER // COL_CHUNK):
        lo = c * COL_CHUNK
        out_ref[:, lo:lo + COL_CHUNK] = _dot(xn, w_ref[:, lo:lo + COL_CHUNK]).astype(BF16)
    nchunks = CONV_DIM // COL_CHUNK
    nbuf = acc_ref.shape[0]

    def xbc_proj(c):
        lo = D_INNER + c * COL_CHUNK
        _store_panels(acc_ref.at[c % nbuf], 0, _dot(xn_ref[...], w_ref[:, lo:lo + COL_CHUNK]))

    for c in range(min(CONV_AHEAD, nchunks)):
        xbc_proj(c)
    for c in range(nchunks):
        lo = c * COL_CHUNK
        if c + CONV_AHEAD < nchunks:
            xbc_proj(c + CONV_AHEAD)
        for p in range(COL_CHUNK // LANES):
            ch = lo + p * LANES
            for r0 in range(0, tm, CONV_ROWS):
                y = _causal_conv(acc_ref.at[c % nbuf], p, cw_ref, cb_ref, ch, SSM_CONV, r0, CONV_ROWS)
                out_ref[r0:r0 + CONV_ROWS, D_INNER + ch:D_INNER + ch + LANES] = _silu(y).astype(BF16)


def _row_specs(tm, d):
    per = tm // HALO
    return (pl.BlockSpec((tm, d), lambda i: (i, 0)),
            pl.BlockSpec((HALO, d), lambda i: (jnp.maximum(i * per - 1, 0), 0)))


def _in_proj(h, seq, norm_w, w_in, conv_w, conv_b, dt_bias):
    t, d = h.shape
    tm = min(ROW_TILE, seq)
    w_main = w_in[:, :D_INNER + CONV_DIM].astype(BF16)
    rep = LANES // SSM_HEADS
    w_dt = jnp.tile(w_in[:, D_INNER + CONV_DIM:], (1, rep)).astype(BF16)
    dtb = jnp.tile(dt_bias, rep).reshape(1, LANES)
    x_spec, xh_spec = _row_specs(tm, d)
    return pl.pallas_call(
        functools.partial(_in_proj_kernel, tm=tm, tiles_per_seq=seq // tm),
        grid=(t // tm,),
        in_specs=[x_spec, xh_spec, _resident((1, d)), _resident(w_main.shape), _resident(w_dt.shape),
                  _resident(conv_w.shape), _resident((1, CONV_DIM)), _resident((1, LANES))],
        out_specs=[pl.BlockSpec((tm, D_INNER + CONV_DIM), lambda i: (i, 0)),
                   pl.BlockSpec((tm, LANES), lambda i: (i, 0))],
        out_shape=[jax.ShapeDtypeStruct((t, D_INNER + CONV_DIM), BF16),
                   jax.ShapeDtypeStruct((t, LANES), F32)],
        scratch_shapes=[pltpu.VMEM((tm + HALO, d), BF16), pltpu.VMEM((CONV_AHEAD + 1, COL_CHUNK // LANES, tm + HALO, LANES), F32)],
        compiler_params=pltpu.CompilerParams(dimension_semantics=("arbitrary",), vmem_limit_bytes=VMEM_LIMIT),
        name="ssm_in_proj",
    )(h, h, norm_w.reshape(1, d), w_main, w_dt, conv_w, conv_b.reshape(1, CONV_DIM), dtb)


def _ssd_kernel(z_ref, xs_ref, b_ref, c_ref, dt_ref, alog_ref, dskip_ref, gnw_ref, ee_ref, ew_ref,
                y_ref, state_ref, yg_ref, e_ref, w_ref):
    @pl.when(pl.program_id(1) == 0)
    def _():
        state_ref[...] = jnp.zeros_like(state_ref)

    n = SSM_CHUNK
    row = lax.broadcasted_iota(jnp.int32, (n, n), 0)
    col = lax.broadcasted_iota(jnp.int32, (n, n), 1)
    causal = col <= row
    low_half = col < SSM_HEAD_DIM
    tri = causal.astype(BF16)

    def hi_lo(x):
        hi = x.astype(BF16)
        return hi, (x - hi.astype(F32)).astype(BF16)

    def decays(u):
        dt = dt_ref[u * n:(u + 1) * n, :]
        ad = dt * (-jnp.exp(alog_ref[...]) * LOG2E)
        a1, a2, a3 = _split3(ad)
        acs = _dot(tri, a1) + _dot(tri, a2) + _dot(tri, a3)
        acs_last = acs[n - 1:n, :]
        src_t = (acs - jnp.log(dt) * LOG2E).T
        e_hi, e_lo = hi_lo(jnp.exp2(acs))
        w_hi, w_lo = hi_lo(jnp.exp2(acs_last - acs) * dt)
        q = SSM_HEADS
        packed = jnp.where(col < q, e_hi, jnp.where(col < 2 * q, e_lo, jnp.where(col < 3 * q, w_hi, w_lo)))
        for c in range(D_INNER // COL_CHUNK):
            lo = c * COL_CHUNK
            e_ref[u, :, lo:lo + COL_CHUNK] = _dot(packed, ee_ref[:, lo:lo + COL_CHUNK])
            w_ref[u, :, lo:lo + COL_CHUNK] = _dot(packed, ew_ref[:, lo:lo + COL_CHUNK])
        return acs, src_t

    terms = [decays(u) for u in range(SSD_SUB)]
    for u in range(SSD_SUB):
        acs, src_t = terms[u]
        rows = slice(u * n, (u + 1) * n)
        for g in range(SSM_GROUPS):
            gl = g * SSM_STATE
            bg = b_ref[rows, gl:gl + SSM_STATE]
            cg = c_ref[rows, gl:gl + SSM_STATE]
            cb = _dot_nt(cg, bg)
            bg_t = bg.astype(F32).T.astype(BF16)
            for p in range(SSM_HEADS // SSM_GROUPS // 2):
                h0 = g * (SSM_HEADS // SSM_GROUPS) + 2 * p
                lo = h0 * SSM_HEAD_DIM
                xs_pair = xs_ref[rows, lo:lo + LANES]
                xs_f = xs_pair.astype(F32)
                ms = []
                for h in (h0, h0 + 1):
                    diff = acs[:, h:h + 1] - src_t[h:h + 1, :]
                    ms.append((cb * jnp.exp2(jnp.where(causal, diff, NEG_BIG))).astype(BF16))
                zero = jnp.zeros_like(xs_pair)
                rhs = jnp.concatenate([jnp.where(low_half, xs_pair, zero), jnp.where(low_half, zero, xs_pair)], axis=0)
                y_diag = _dot(jnp.concatenate(ms, axis=1), rhs)
                s_pair = state_ref[:, lo:lo + LANES]
                y_off = _dot(cg, s_pair.astype(BF16)) * e_ref[u, :, lo:lo + LANES]
                xw = (xs_f * w_ref[u, :, lo:lo + LANES]).astype(BF16)
                state_ref[:, lo:lo + LANES] = s_pair * e_ref[u, n - 1:n, lo:lo + LANES] + _dot(bg_t, xw)
                y = y_diag + y_off + dskip_ref[:, lo:lo + LANES] * xs_f
                yg_ref[u, :, lo:lo + LANES] = y * _silu(z_ref[rows, lo:lo + LANES].astype(F32))
            gw = D_INNER // SSM_GROUPS
            yg = yg_ref[u, :, g * gw:(g + 1) * gw]
            y_ref[rows, g * gw:(g + 1) * gw] = _rms(yg, gnw_ref[:, g * gw:(g + 1) * gw]).astype(BF16)


def _head_spread(first_group):
    k = jnp.arange(LANES)[:, None]
    head = jnp.arange(D_INNER)[None, :] // SSM_HEAD_DIM
    group = k // SSM_HEADS
    return ((k % SSM_HEADS == head) & (group >= first_group) & (group < first_group + 2)).astype(BF16)


def _ssd(zx, dt, bsz, seq, a_log, d_skip, gate_norm_w):
    t = zx.shape[0]
    rows = SSD_SUB * SSM_CHUNK
    nc = seq // rows
    alog = jnp.tile(a_log, LANES // SSM_HEADS).reshape(1, LANES)
    dskip = jnp.repeat(d_skip, SSM_HEAD_DIM).reshape(1, D_INNER)
    rowblk = lambda b, c: b * nc + c
    return pl.pallas_call(
        _ssd_kernel,
        grid=(bsz, nc),
        in_specs=[pl.BlockSpec((rows, D_INNER), lambda b, c: (rowblk(b, c), 0)),
                  pl.BlockSpec((rows, D_INNER), lambda b, c: (rowblk(b, c), 1)),
                  pl.BlockSpec((rows, GN), lambda b, c: (rowblk(b, c), 2 * D_INNER // GN)),
                  pl.BlockSpec((rows, GN), lambda b, c: (rowblk(b, c), 2 * D_INNER // GN + 1)),
                  pl.BlockSpec((rows, LANES), lambda b, c: (rowblk(b, c), 0)),
                  _resident((1, LANES)), _resident((1, D_INNER)), _resident((1, D_INNER)),
                  _resident((LANES, D_INNER)), _resident((LANES, D_INNER))],
        out_specs=pl.BlockSpec((rows, D_INNER), lambda b, c: (rowblk(b, c), 0)),
        out_shape=jax.ShapeDtypeStruct((t, D_INNER), BF16),
        scratch_shapes=[pltpu.VMEM((SSM_STATE, D_INNER), F32)] + [pltpu.VMEM((SSD_SUB, SSM_CHUNK, D_INNER), F32)] * 3,
        compiler_params=pltpu.CompilerParams(dimension_semantics=("arbitrary", "arbitrary"),
                                             vmem_limit_bytes=VMEM_LIMIT),
        name="ssd_scan",
    )(zx, zx, zx, zx, dt, alog, dskip, gate_norm_w.reshape(1, D_INNER), _head_spread(0), _head_spread(2))


def _ffn_kernel(a_ref, wp_ref, res_ref, nw_ref, wup_ref, cw_ref, cb_ref, wd_ref, fnw_ref,
                out_ref, xn_ref, tail_ref, h_ref, hid_ref, *, tm, tiles_per_seq, final_norm):
    first = (pl.program_id(0) % tiles_per_seq) == 0
    for c in range(D_MODEL // COL_CHUNK):
        lo = c * COL_CHUNK
        out_ref[:, lo:lo + COL_CHUNK] = res_ref[:, lo:lo + COL_CHUNK] + _dot(a_ref[...], wp_ref[:, lo:lo + COL_CHUNK])

    @pl.when(first)
    def _():
        tail_ref[...] = jnp.zeros_like(tail_ref)

    xn_ref[:HALO, :] = tail_ref[...]
    xn_ref[HALO:, :] = _rms(out_ref[...], nw_ref[...]).astype(BF16)
    tail_ref[...] = xn_ref[tm:, :]
    fc = FFN_CHUNK
    nchunks = D_FF // fc

    nbuf = h_ref.shape[0]

    def up_proj(c):
        buf = h_ref.at[c % nbuf]
        _store_panels(buf, 0, _dot(xn_ref[...], wup_ref[:, c * fc:(c + 1) * fc]))
        _store_panels(buf, fc // LANES, _dot(xn_ref[...], wup_ref[:, D_FF + c * fc:D_FF + (c + 1) * fc]))

    for c in range(min(CONV_AHEAD, nchunks)):
        up_proj(c)
    for c in range(nchunks):
        if c + CONV_AHEAD < nchunks:
            up_proj(c + CONV_AHEAD)
        buf = h_ref.at[c % nbuf]
        for p in range(fc // LANES):
            ch = c * fc + p * LANES
            for r0 in range(0, tm, CONV_ROWS):
                gate = _causal_conv(buf, p, cw_ref, cb_ref, ch, FFN_CONV, r0, CONV_ROWS)
                value = _causal_conv(buf, fc // LANES + p, cw_ref, cb_ref, D_FF + ch, FFN_CONV, r0, CONV_ROWS)
                hid_ref[r0:r0 + CONV_ROWS, ch:ch + LANES] = (_silu(gate) * value).astype(BF16)
    for c in range(D_MODEL // COL_CHUNK):
        lo = c * COL_CHUNK
        out_ref[:, lo:lo + COL_CHUNK] += _dot(hid_ref[...], wd_ref[:, lo:lo + COL_CHUNK])
    if final_norm:
        out_ref[...] = _rms(out_ref[...], fnw_ref[...])


def _proj_ffn(a, w_proj, res, seq, norm_w, w_up, conv_w, conv_b, w_down, final_norm_w=None):
    t, k = a.shape
    d = res.shape[1]
    tm = min(ROW_TILE, seq)
    final_norm = final_norm_w is not None
    fnw = (final_norm_w if final_norm else jnp.ones((d,), F32)).reshape(1, d)
    return pl.pallas_call(
        functools.partial(_ffn_kernel, tm=tm, tiles_per_seq=seq // tm, final_norm=final_norm),
        grid=(t // tm,),
        in_specs=[pl.BlockSpec((tm, k), lambda i: (i, 0)), _resident((k, d)), pl.BlockSpec((tm, d), lambda i: (i, 0)),
                  _resident((1, d)), _resident((d, 2 * D_FF)), _resident((FFN_CONV, 2 * D_FF)),
                  _resident((1, 2 * D_FF)), _resident((D_FF, d)), _resident((1, d))],
        out_specs=pl.BlockSpec((tm, d), lambda i: (i, 0)),
        out_shape=jax.ShapeDtypeStruct((t, d), F32),
        scratch_shapes=[pltpu.VMEM((tm + HALO, d), BF16), pltpu.VMEM((HALO, d), BF16),
                        pltpu.VMEM((CONV_AHEAD + 1, 2 * FFN_CHUNK // LANES, tm + HALO, LANES), F32),
                        pltpu.VMEM((tm, D_FF), BF16)],
        compiler_params=pltpu.CompilerParams(dimension_semantics=("arbitrary",), vmem_limit_bytes=VMEM_LIMIT),
        name="proj_conv_ffn",
    )(a, w_proj.astype(BF16), res, norm_w.reshape(1, d), w_up.astype(BF16), conv_w,
      conv_b.reshape(1, 2 * D_FF), w_down.astype(BF16), fnw)


def _qkv_kernel(x_ref, qnw_ref, kvnw_ref, w_ref, out_ref):
    x = x_ref[...]
    xhat = x * lax.rsqrt(jnp.mean(x * x, axis=-1, keepdims=True) + EPS)
    xq = (xhat * qnw_ref[...]).astype(BF16)
    xkv = (xhat * kvnw_ref[...]).astype(BF16)
    for c in range(3 * D_MODEL // COL_CHUNK):
        lo = c * COL_CHUNK
        if lo < D_MODEL:
            out_ref[:, lo:lo + COL_CHUNK] = (_dot(xq, w_ref[:, lo:lo + COL_CHUNK]) * Q_SCALE).astype(BF16)
        else:
            out_ref[:, lo:lo + COL_CHUNK] = _dot(xkv, w_ref[:, lo:lo + COL_CHUNK]).astype(BF16)


def _qkv_proj(h, seq, q_norm_w, kv_norm_w, w_q, w_k, w_v):
    t, d = h.shape
    tm = min(ROW_TILE, seq)
    w = jnp.concatenate([w_q, w_k, w_v], axis=1).astype(BF16)
    return pl.pallas_call(
        _qkv_kernel,
        grid=(t // tm,),
        in_specs=[pl.BlockSpec((tm, d), lambda i: (i, 0)), _resident((1, d)), _resident((1, d)),
                  _resident((d, 3 * d))],
        out_specs=pl.BlockSpec((tm, 3 * d), lambda i: (i, 0)),
        out_shape=jax.ShapeDtypeStruct((t, 3 * d), BF16),
        compiler_params=pltpu.CompilerParams(dimension_semantics=("arbitrary",), vmem_limit_bytes=VMEM_LIMIT),
        name="qkv_proj",
    )(h, q_norm_w.reshape(1, d), kv_norm_w.reshape(1, d), w)


def _attn_kernel(q_ref, k_ref, v_ref, tt_ref, out_ref, ks_ref, vs_ref):
    i = pl.program_id(2)
    n = SB_BLOCK
    nblk = ks_ref.shape[0] - 1
    row = lax.broadcasted_iota(jnp.int32, (n, n), 0)
    col = lax.broadcasted_iota(jnp.int32, (n, n), 1)
    low_half = col < SB_HEAD_DIM
    before = col < row

    @pl.when(i == 0)
    def _():
        def fill(j, _):
            sl = pl.ds(pl.multiple_of(j * n, n), n)
            kj = k_ref[sl, :]
            vj = v_ref[sl, :]
            zero = jnp.zeros_like(kj)
            ks_ref[j] = jnp.concatenate([jnp.where(low_half, kj, zero), jnp.where(low_half, zero, kj)], axis=0)
            vs_ref[j] = jnp.concatenate([jnp.where(low_half, vj, zero), jnp.where(low_half, zero, vj)], axis=0)
            return 0
        lax.fori_loop(0, nblk, fill, 0)
        ks_ref[nblk] = jnp.zeros(ks_ref.shape[1:], BF16)
        vs_ref[nblk] = jnp.zeros(vs_ref.shape[1:], BF16)

    tt = tt_ref[...]
    tiles = [i * ATTN_TILES + u for u in range(ATTN_TILES)]
    qss = [q_ref[u * n:(u + 1) * n, :] for u in range(ATTN_TILES)]

    def sweep(us, ts, states, diagonal):
        work = [(u, g) for u in range(len(us)) for g in range(ATTN_GROUP)]
        jcs, z2s, sp2s, laters = {}, {}, {}, {}
        for u, g in work:
            j = tiles[us[u]] - (ts[u] * ATTN_GROUP + g)
            jcs[u, g] = jnp.where(j >= 0, j, nblk)
            z2s[u, g] = _dot_nt(qss[us[u]], ks_ref[jcs[u, g]])
        for key in work:
            sp2s[key] = _softplus_log2(z2s[key])
        carries = [list(st[0]) for st in states]
        for u, g in work:
            for h in range(2):
                sp = sp2s[u, g][:, h * n:(h + 1) * n]
                if diagonal and g == 0:
                    sp = jnp.where(before, sp, 0.0)
                cs = _dot(sp.astype(BF16), tt)
                if carries[u][h] is None:
                    laters[u, g, h], carries[u][h] = cs[:, :n], cs[:, n:]
                else:
                    laters[u, g, h] = cs[:, :n] + carries[u][h]
                    carries[u][h] = carries[u][h] + cs[:, n:]
        mins = [jnp.min(jnp.minimum(c[0], c[1])) for c in carries]
        accs = [st[1] for st in states]
        for u, g in work:
            atts = []
            for h in range(2):
                sl = slice(h * n, (h + 1) * n)
                att = jnp.exp2(z2s[u, g][:, sl] - sp2s[u, g][:, sl] - laters[u, g, h])
                if diagonal and g == 0:
                    att = jnp.where(before, att, 0.0)
                atts.append(att.astype(BF16))
            pv = _dot(jnp.concatenate(atts, axis=1), vs_ref[jcs[u, g]])
            accs[u] = pv if accs[u] is None else accs[u] + pv
        return [(carries[u], accs[u], mins[u]) for u in range(len(us))]

    us = list(range(ATTN_TILES))
    first = sweep(us, [0] * ATTN_TILES, [([None, None], None)] * ATTN_TILES, True)
    for u in us:
        carries, acc, min_carry = first[u]
        ngroups = (tiles[u] + ATTN_GROUP) // ATTN_GROUP

        def more(state, ngroups=ngroups):
            return jnp.logical_and(state[0] < ngroups, state[1] <= ATTN_STOP)

        def group(state, u=u):
            t, _, c0, c1, acc = state
            (carries, acc, min_carry), = sweep([u], [t], [([c0, c1], acc)], False)
            return t + 1, min_carry, carries[0], carries[1], acc

        state = lax.while_loop(more, group, (jnp.int32(1), min_carry, carries[0], carries[1], acc))
        out_ref[u * n:(u + 1) * n, :] = state[4].astype(BF16)


def _attention(qkv, bsz, seq):
    t = qkv.shape[0]
    n = SB_BLOCK
    nq = seq // n
    pairs = SB_HEADS // 2
    r = jnp.arange(n)
    tri = (r[:, None] > r[None, :]).astype(BF16)
    tt = jnp.concatenate([tri, jnp.ones((n, n), BF16)], axis=1)
    return pl.pallas_call(
        _attn_kernel,
        grid=(bsz, pairs, nq // ATTN_TILES),
        in_specs=[pl.BlockSpec((ATTN_TILES * n, LANES), lambda b, p, i: (b * (nq // ATTN_TILES) + i, p)),
                  pl.BlockSpec((seq, LANES), lambda b, p, i: (b, pairs + p)),
                  pl.BlockSpec((seq, LANES), lambda b, p, i: (b, 2 * pairs + p)),
                  _resident((n, 2 * n))],
        out_specs=pl.BlockSpec((ATTN_TILES * n, LANES), lambda b, p, i: (b * (nq // ATTN_TILES) + i, p)),
        out_shape=jax.ShapeDtypeStruct((t, SB_HEADS * SB_HEAD_DIM), BF16),
        scratch_shapes=[pltpu.VMEM((nq + 1, 2 * n, LANES), BF16), pltpu.VMEM((nq + 1, 2 * n, LANES), BF16)],
        compiler_params=pltpu.CompilerParams(dimension_semantics=("arbitrary", "arbitrary", "arbitrary"),
                                             vmem_limit_bytes=VMEM_LIMIT),
        name="stick_breaking_attention",
    )(qkv, qkv, qkv, tt)


def kernel(x, ssm_norm_w, ssm_in_w, ssm_conv_w, ssm_conv_b, ssm_dt_bias, ssm_a_log, ssm_d, ssm_gate_norm_w, ssm_out_w, kv_norm_w, w_k, w_v, attn_norm_w, w_q, w_o, ffn_norm_w, ffn_up_w, ffn_conv_w, ffn_conv_b, ffn_down_w, final_norm_w):
    bsz, seq, d = x.shape
    h = x.reshape(bsz * seq, d)
    zx, dt = _in_proj(h, seq, ssm_norm_w[0], ssm_in_w[0], ssm_conv_w[0], ssm_conv_b[0], ssm_dt_bias[0])
    y = _ssd(zx, dt, bsz, seq, ssm_a_log[0], ssm_d[0], ssm_gate_norm_w[0])
    h = _proj_ffn(y, ssm_out_w[0], h, seq, ffn_norm_w[0], ffn_up_w[0], ffn_conv_w[0], ffn_conv_b[0], ffn_down_w[0])
    qkv = _qkv_proj(h, seq, attn_norm_w[0], kv_norm_w, w_q[0], w_k, w_v)
    o = _attention(qkv, bsz, seq)
    h = _proj_ffn(o, w_o[0], h, seq, ffn_norm_w[1], ffn_up_w[1], ffn_conv_w[1], ffn_conv_b[1], ffn_down_w[1],
                  final_norm_w)
    return h.reshape(bsz, seq, d)
```

```python
import functools
import math

import jax
import jax.numpy as jnp
from jax import lax
from jax.experimental import pallas as pl
from jax.experimental.pallas import tpu as pltpu

D_MODEL = 1024
D_INNER = 2048
SSM_HEADS = 32
SSM_HEAD_DIM = 64
SSM_GROUPS = 4
SSM_STATE = 128
SSM_CONV = 4
SSM_CHUNK = 128
GN = SSM_GROUPS * SSM_STATE
CONV_DIM = D_INNER + 2 * GN
SB_HEADS = 16
SB_HEAD_DIM = 64
SB_BLOCK = 128
D_FF = 2816
FFN_CONV = 3
EPS = 1e-6

LANES = 128
HALO = 16
ROW_TILE = 512
COL_CHUNK = 512
FFN_CHUNK = 256
VMEM_LIMIT = 56 * 1024 * 1024
NEG_BIG = -1e30
LOG2E = math.log2(math.e)
CONV_ROWS = 128
CONV_AHEAD = 2
SSD_SUB = 2
ATTN_GROUP = 3
ATTN_TILES = 16
Q_SCALE = LOG2E / math.sqrt(SB_HEAD_DIM)
ATTN_STOP = 104.0 * LOG2E

F32 = jnp.float32
BF16 = jnp.bfloat16


def _rms(x, w):
    ms = jnp.mean(x * x, axis=-1, keepdims=True)
    return x * lax.rsqrt(ms + EPS) * w


def _softplus(x):
    return jnp.maximum(x, 0.0) + jnp.log(1.0 + jnp.exp(-jnp.abs(x)))


def _softplus_log2(x):
    return jnp.maximum(x, 0.0) + jnp.log(1.0 + jnp.exp2(-jnp.abs(x))) * LOG2E


def _silu(x):
    return x / (1.0 + jnp.exp(-x))


def _dot(a, b):
    return jnp.dot(a, b, preferred_element_type=F32)


def _dot_nt(a, b):
    return lax.dot_general(a, b, (((1,), (1,)), ((), ())), preferred_element_type=F32)


def _split3(x):
    h1 = x.astype(BF16)
    r1 = x - h1.astype(F32)
    h2 = r1.astype(BF16)
    h3 = (r1 - h2.astype(F32)).astype(BF16)
    return h1, h2, h3


def _resident(shape):
    return pl.BlockSpec(shape, lambda *_: (0,) * len(shape), pipeline_mode=pl.Buffered(1))


def _fill_normed(xn_ref, x_ref, xh_ref, nw, first):
    xn_ref[HALO:, :] = _rms(x_ref[...], nw).astype(BF16)
    xh = _rms(xh_ref[...], nw)
    xn_ref[:HALO, :] = jnp.where(first, 0.0, xh).astype(BF16)


def _store_panels(buf, first_panel, res):
    for p in range(res.shape[1] // LANES):
        buf[first_panel + p] = res[:, p * LANES:(p + 1) * LANES]


def _causal_conv(buf, panel, cw_ref, cb_ref, ch, taps, r0, rows):
    y = cb_ref[:, ch:ch + LANES]
    for k in range(taps):
        y = y + buf[panel, pl.ds(HALO - (taps - 1) + k + r0, rows), :] * cw_ref[k:k + 1, ch:ch + LANES]
    return y


def _in_proj_kernel(x_ref, xh_ref, nw_ref, w_ref, wdt_ref, cw_ref, cb_ref, dtb_ref,
                    out_ref, dt_ref, xn_ref, acc_ref, *, tm, tiles_per_seq):
    first = (pl.program_id(0) % tiles_per_seq) == 0
    _fill_normed(xn_ref, x_ref, xh_ref, nw_ref[...], first)
    xn = xn_ref[HALO:, :]
    dt_ref[...] = _softplus(_dot(xn, wdt_ref[...]) + dtb_ref[...])
    for c in range(D_INNER // COL_CHUNK):
        lo = c * COL_CHUNK
        out_ref[:, lo:lo + COL_CHUNK] = _dot(xn, w_ref[:, lo:lo + COL_CHUNK]).astype(BF16)
    nchunks = CONV_DIM // COL_CHUNK
    nbuf = acc_ref.shape[0]

    def xbc_proj(c):
        lo = D_INNER + c * COL_CHUNK
        _store_panels(acc_ref.at[c % nbuf], 0, _dot(xn_ref[...], w_ref[:, lo:lo + COL_CHUNK]))

    for c in range(min(CONV_AHEAD, nchunks)):
        xbc_proj(c)
    for c in range(nchunks):
        lo = c * COL_CHUNK
        if c + CONV_AHEAD < nchunks:
            xbc_proj(c + CONV_AHEAD)
        for p in range(COL_CHUNK // LANES):
            ch = lo + p * LANES
            for r0 in range(0, tm, CONV_ROWS):
                y = _causal_conv(acc_ref.at[c % nbuf], p, cw_ref, cb_ref, ch, SSM_CONV, r0, CONV_ROWS)
                out_ref[r0:r0 + CONV_ROWS, D_INNER + ch:D_INNER + ch + LANES] = _silu(y).astype(BF16)


def _row_specs(tm, d):
    per = tm // HALO
    return (pl.BlockSpec((tm, d), lambda i: (i, 0)),
            pl.BlockSpec((HALO, d), lambda i: (jnp.maximum(i * per - 1, 0), 0)))


def _in_proj(h, seq, norm_w, w_in, conv_w, conv_b, dt_bias):
    t, d = h.shape
    tm = min(ROW_TILE, seq)
    w_main = w_in[:, :D_INNER + CONV_DIM].astype(BF16)
    rep = LANES // SSM_HEADS
    w_dt = jnp.tile(w_in[:, D_INNER + CONV_DIM:], (1, rep)).astype(BF16)
    dtb = jnp.tile(dt_bias, rep).reshape(1, LANES)
    x_spec, xh_spec = _row_specs(tm, d)
    return pl.pallas_call(
        functools.partial(_in_proj_kernel, tm=tm, tiles_per_seq=seq // tm),
        grid=(t // tm,),
        in_specs=[x_spec, xh_spec, _resident((1, d)), _resident(w_main.shape), _resident(w_dt.shape),
                  _resident(conv_w.shape), _resident((1, CONV_DIM)), _resident((1, LANES))],
        out_specs=[pl.BlockSpec((tm, D_INNER + CONV_DIM), lambda i: (i, 0)),
                   pl.BlockSpec((tm, LANES), lambda i: (i, 0))],
        out_shape=[jax.ShapeDtypeStruct((t, D_INNER + CONV_DIM), BF16),
                   jax.ShapeDtypeStruct((t, LANES), F32)],
        scratch_shapes=[pltpu.VMEM((tm + HALO, d), BF16), pltpu.VMEM((CONV_AHEAD + 1, COL_CHUNK // LANES, tm + HALO, LANES), F32)],
        compiler_params=pltpu.CompilerParams(dimension_semantics=("arbitrary",), vmem_limit_bytes=VMEM_LIMIT),
        name="ssm_in_proj",
    )(h, h, norm_w.reshape(1, d), w_main, w_dt, conv_w, conv_b.reshape(1, CONV_DIM), dtb)


def _ssd_kernel(z_ref, xs_ref, b_ref, c_ref, dt_ref, alog_ref, dskip_ref, gnw_ref, ee_ref, ew_ref,
                y_ref, state_ref, yg_ref, e_ref, w_ref):
    @pl.when(pl.program_id(1) == 0)
    def _():
        state_ref[...] = jnp.zeros_like(state_ref)

    n = SSM_CHUNK
    row = lax.broadcasted_iota(jnp.int32, (n, n), 0)
    col = lax.broadcasted_iota(jnp.int32, (n, n), 1)
    causal = col <= row
    low_half = col < SSM_HEAD_DIM
    tri = causal.astype(BF16)

    def hi_lo(x):
        hi = x.astype(BF16)
        return hi, (x - hi.astype(F32)).astype(BF16)

    def decays(u):
        dt = dt_ref[u * n:(u + 1) * n, :]
        ad = dt * (-jnp.exp(alog_ref[...]) * LOG2E)
        a1, a2, a3 = _split3(ad)
        acs = _dot(tri, a1) + _dot(tri, a2) + _dot(tri, a3)
        acs_last = acs[n - 1:n, :]
        src_t = (acs - jnp.log(dt) * LOG2E).T
        e_hi, e_lo = hi_lo(jnp.exp2(acs))
        w_hi, w_lo = hi_lo(jnp.exp2(acs_last - acs) * dt)
        q = SSM_HEADS
        packed = jnp.where(col < q, e_hi, jnp.where(col < 2 * q, e_lo, jnp.where(col < 3 * q, w_hi, w_lo)))
        for c in range(D_INNER // COL_CHUNK):
            lo = c * COL_CHUNK
            e_ref[u, :, lo:lo + COL_CHUNK] = _dot(packed, ee_ref[:, lo:lo + COL_CHUNK])
            w_ref[u, :, lo:lo + COL_CHUNK] = _dot(packed, ew_ref[:, lo:lo + COL_CHUNK])
        return acs, src_t

    terms = [decays(u) for u in range(SSD_SUB)]
    for u in range(SSD_SUB):
        acs, src_t = terms[u]
        rows = slice(u * n, (u + 1) * n)
        for g in range(SSM_GROUPS):
            gl = g * SSM_STATE
            bg = b_ref[rows, gl:gl + SSM_STATE]
            cg = c_ref[rows, gl:gl + SSM_STATE]
            cb = _dot_nt(cg, bg)
            bg_t = bg.astype(F32).T.astype(BF16)
            for p in range(SSM_HEADS // SSM_GROUPS // 2):
                h0 = g * (SSM_HEADS // SSM_GROUPS) + 2 * p
                lo = h0 * SSM_HEAD_DIM
                xs_pair = xs_ref[rows, lo:lo + LANES]
                xs_f = xs_pair.astype(F32)
                ms = []
                for h in (h0, h0 + 1):
                    diff = acs[:, h:h + 1] - src_t[h:h + 1, :]
                    ms.append((cb * jnp.exp2(jnp.where(causal, diff, NEG_BIG))).astype(BF16))
                zero = jnp.zeros_like(xs_pair)
                rhs = jnp.concatenate([jnp.where(low_half, xs_pair, zero), jnp.where(low_half, zero, xs_pair)], axis=0)
                y_diag = _dot(jnp.concatenate(ms, axis=1), rhs)
                s_pair = state_ref[:, lo:lo + LANES]
                y_off = _dot(cg, s_pair.astype(BF16)) * e_ref[u, :, lo:lo + LANES]
                xw = (xs_f * w_ref[u, :, lo:lo + LANES]).astype(BF16)
                state_ref[:, lo:lo + LANES] = s_pair * e_ref[u, n - 1:n, lo:lo + LANES] + _dot(bg_t, xw)
                y = y_diag + y_off + dskip_ref[:, lo:lo + LANES] * xs_f
                yg_ref[u, :, lo:lo + LANES] = y * _silu(z_ref[rows, lo:lo + LANES].astype(F32))
            gw = D_INNER // SSM_GROUPS
            yg = yg_ref[u, :, g * gw:(g + 1) * gw]
            y_ref[rows, g * gw:(g + 1) * gw] = _rms(yg, gnw_ref[:, g * gw:(g + 1) * gw]).astype(BF16)


def _head_spread(first_group):
    k = jnp.arange(LANES)[:, None]
    head = jnp.arange(D_INNER)[None, :] // SSM_HEAD_DIM
    group = k // SSM_HEADS
    return ((k % SSM_HEADS == head) & (group >= first_group) & (group < first_group + 2)).astype(BF16)


def _ssd(zx, dt, bsz, seq, a_log, d_skip, gate_norm_w):
    t = zx.shape[0]
    rows = SSD_SUB * SSM_CHUNK
    nc = seq // rows
    alog = jnp.tile(a_log, LANES // SSM_HEADS).reshape(1, LANES)
    dskip = jnp.repeat(d_skip, SSM_HEAD_DIM).reshape(1, D_INNER)
    rowblk = lambda b, c: b * nc + c
    return pl.pallas_call(
        _ssd_kernel,
        grid=(bsz, nc),
        in_specs=[pl.BlockSpec((rows, D_INNER), lambda b, c: (rowblk(b, c), 0)),
                  pl.BlockSpec((rows, D_INNER), lambda b, c: (rowblk(b, c), 1)),
                  pl.BlockSpec((rows, GN), lambda b, c: (rowblk(b, c), 2 * D_INNER // GN)),
                  pl.BlockSpec((rows, GN), lambda b, c: (rowblk(b, c), 2 * D_INNER // GN + 1)),
                  pl.BlockSpec((rows, LANES), lambda b, c: (rowblk(b, c), 0)),
                  _resident((1, LANES)), _resident((1, D_INNER)), _resident((1, D_INNER)),
                  _resident((LANES, D_INNER)), _resident((LANES, D_INNER))],
        out_specs=pl.BlockSpec((rows, D_INNER), lambda b, c: (rowblk(b, c), 0)),
        out_shape=jax.ShapeDtypeStruct((t, D_INNER), BF16),
        scratch_shapes=[pltpu.VMEM((SSM_STATE, D_INNER), F32)] + [pltpu.VMEM((SSD_SUB, SSM_CHUNK, D_INNER), F32)] * 3,
        compiler_params=pltpu.CompilerParams(dimension_semantics=("arbitrary", "arbitrary"),
                                             vmem_limit_bytes=VMEM_LIMIT),
        name="ssd_scan",
    )(zx, zx, zx, zx, dt, alog, dskip, gate_norm_w.reshape(1, D_INNER), _head_spread(0), _head_spread(2))


def _ffn_kernel(a_ref, wp_ref, res_ref, nw_ref, wup_ref, cw_ref, cb_ref, wd_ref, fnw_ref,
                out_ref, xn_ref, tail_ref, h_ref, hid_ref, *, tm, tiles_per_seq, final_norm):
    first = (pl.program_id(0) % tiles_per_seq) == 0
    for c in range(D_MODEL // COL_CHUNK):
        lo = c * COL_CHUNK
        out_ref[:, lo:lo + COL_CHUNK] = res_ref[:, lo:lo + COL_CHUNK] + _dot(a_ref[...], wp_ref[:, lo:lo + COL_CHUNK])

    @pl.when(first)
    def _():
        tail_ref[...] = jnp.zeros_like(tail_ref)

    xn_ref[:HALO, :] = tail_ref[...]
    xn_ref[HALO:, :] = _rms(out_ref[...], nw_ref[...]).astype(BF16)
    tail_ref[...] = xn_ref[tm:, :]
    fc = FFN_CHUNK
    nchunks = D_FF // fc

    nbuf = h_ref.shape[0]

    def up_proj(c):
        buf = h_ref.at[c % nbuf]
        _store_panels(buf, 0, _dot(xn_ref[...], wup_ref[:, c * fc:(c + 1) * fc]))
        _store_panels(buf, fc // LANES, _dot(xn_ref[...], wup_ref[:, D_FF + c * fc:D_FF + (c + 1) * fc]))

    for c in range(min(CONV_AHEAD, nchunks)):
        up_proj(c)
    for c in range(nchunks):
        if c + CONV_AHEAD < nchunks:
            up_proj(c + CONV_AHEAD)
        buf = h_ref.at[c % nbuf]
        for p in range(fc // LANES):
            ch = c * fc + p * LANES
            for r0 in range(0, tm, CONV_ROWS):
                gate = _causal_conv(buf, p, cw_ref, cb_ref, ch, FFN_CONV, r0, CONV_ROWS)
                value = _causal_conv(buf, fc // LANES + p, cw_ref, cb_ref, D_FF + ch, FFN_CONV, r0, CONV_ROWS)
                hid_ref[r0:r0 + CONV_ROWS, ch:ch + LANES] = (_silu(gate) * value).astype(BF16)
    for c in range(D_MODEL // COL_CHUNK):
        lo = c * COL_CHUNK
        out_ref[:, lo:lo + COL_CHUNK] += _dot(hid_ref[...], wd_ref[:, lo:lo + COL_CHUNK])
    if final_norm:
        out_ref[...] = _rms(out_ref[...], fnw_ref[...])


def _proj_ffn(a, w_proj, res, seq, norm_w, w_up, conv_w, conv_b, w_down, final_norm_w=None):
    t, k = a.shape
    d = res.shape[1]
    tm = min(ROW_TILE, seq)
    final_norm = final_norm_w is not None
    fnw = (final_norm_w if final_norm else jnp.ones((d,), F32)).reshape(1, d)
    return pl.pallas_call(
        functools.partial(_ffn_kernel, tm=tm, tiles_per_seq=seq // tm, final_norm=final_norm),
        grid=(t // tm,),
        in_specs=[pl.BlockSpec((tm, k), lambda i: (i, 0)), _resident((k, d)), pl.BlockSpec((tm, d), lambda i: (i, 0)),
                  _resident((1, d)), _resident((d, 2 * D_FF)), _resident((FFN_CONV, 2 * D_FF)),
                  _resident((1, 2 * D_FF)), _resident((D_FF, d)), _resident((1, d))],
        out_specs=pl.BlockSpec((tm, d), lambda i: (i, 0)),
        out_shape=jax.ShapeDtypeStruct((t, d), F32),
        scratch_shapes=[pltpu.VMEM((tm + HALO, d), BF16), pltpu.VMEM((HALO, d), BF16),
                        pltpu.VMEM((CONV_AHEAD + 1, 2 * FFN_CHUNK // LANES, tm + HALO, LANES), F32),
                        pltpu.VMEM((tm, D_FF), BF16)],
        compiler_params=pltpu.CompilerParams(dimension_semantics=("arbitrary",), vmem_limit_bytes=VMEM_LIMIT),
        name="proj_conv_ffn",
    )(a, w_proj.astype(BF16), res, norm_w.reshape(1, d), w_up.astype(BF16), conv_w,
      conv_b.reshape(1, 2 * D_FF), w_down.astype(BF16), fnw)


def _qkv_kernel(x_ref, qnw_ref, kvnw_ref, w_ref, out_ref):
    x = x_ref[...]
    xhat = x * lax.rsqrt(jnp.mean(x * x, axis=-1, keepdims=True) + EPS)
    xq = (xhat * qnw_ref[...]).astype(BF16)
    xkv = (xhat * kvnw_ref[...]).astype(BF16)
    for c in range(3 * D_MODEL // COL_CHUNK):
        lo = c * COL_CHUNK
        if lo < D_MODEL:
            out_ref[:, lo:lo + COL_CHUNK] = (_dot(xq, w_ref[:, lo:lo + COL_CHUNK]) * Q_SCALE).astype(BF16)
        else:
            out_ref[:, lo:lo + COL_CHUNK] = _dot(xkv, w_ref[:, lo:lo + COL_CHUNK]).astype(BF16)


def _qkv_proj(h, seq, q_norm_w, kv_norm_w, w_q, w_k, w_v):
    t, d = h.shape
    tm = min(ROW_TILE, seq)
    w = jnp.concatenate([w_q, w_k, w_v], axis=1).astype(BF16)
    return pl.pallas_call(
        _qkv_kernel,
        grid=(t // tm,),
        in_specs=[pl.BlockSpec((tm, d), lambda i: (i, 0)), _resident((1, d)), _resident((1, d)),
                  _resident((d, 3 * d))],
        out_specs=pl.BlockSpec((tm, 3 * d), lambda i: (i, 0)),
        out_shape=jax.ShapeDtypeStruct((t, 3 * d), BF16),
        compiler_params=pltpu.CompilerParams(dimension_semantics=("arbitrary",), vmem_limit_bytes=VMEM_LIMIT),
        name="qkv_proj",
    )(h, q_norm_w.reshape(1, d), kv_norm_w.reshape(1, d), w)


def _attn_kernel(q_ref, k_ref, v_ref, tt_ref, out_ref, ks_ref, vs_ref):
    i = pl.program_id(2)
    n = SB_BLOCK
    nblk = ks_ref.shape[0] - 1
    row = lax.broadcasted_iota(jnp.int32, (n, n), 0)
    col = lax.broadcasted_iota(jnp.int32, (n, n), 1)
    low_half = col < SB_HEAD_DIM
    before = col < row

    @pl.when(i == 0)
    def _():
        def fill(j, _):
            sl = pl.ds(pl.multiple_of(j * n, n), n)
            kj = k_ref[sl, :]
            vj = v_ref[sl, :]
            zero = jnp.zeros_like(kj)
            ks_ref[j] = jnp.concatenate([jnp.where(low_half, kj, zero), jnp.where(low_half, zero, kj)], axis=0)
            vs_ref[j] = jnp.concatenate([jnp.where(low_half, vj, zero), jnp.where(low_half, zero, vj)], axis=0)
            return 0
        lax.fori_loop(0, nblk, fill, 0)
        ks_ref[nblk] = jnp.zeros(ks_ref.shape[1:], BF16)
        vs_ref[nblk] = jnp.zeros(vs_ref.shape[1:], BF16)

    tt = tt_ref[...]
    tiles = [i * ATTN_TILES + u for u in range(ATTN_TILES)]
    qss = [q_ref[u * n:(u + 1) * n, :] for u in range(ATTN_TILES)]

    def sweep(us, ts, states, diagonal):
        work = [(u, g) for u in range(len(us)) for g in range(ATTN_GROUP)]
        jcs, z2s, sp2s, laters = {}, {}, {}, {}
        for u, g in work:
            j = tiles[us[u]] - (ts[u] * ATTN_GROUP + g)
            jcs[u, g] = jnp.where(j >= 0, j, nblk)
            z2s[u, g] = _dot_nt(qss[us[u]], ks_ref[jcs[u, g]])
        for key in work:
            sp2s[key] = _softplus_log2(z2s[key])
        carries = [list(st[0]) for st in states]
        for u, g in work:
            for h in range(2):
                sp = sp2s[u, g][:, h * n:(h + 1) * n]
                if diagonal and g == 0:
                    sp = jnp.where(before, sp, 0.0)
                cs = _dot(sp.astype(BF16), tt)
                if carries[u][h] is None:
                    laters[u, g, h], carries[u][h] = cs[:, :n], cs[:, n:]
                else:
                    laters[u, g, h] = cs[:, :n] + carries[u][h]
                    carries[u][h] = carries[u][h] + cs[:, n:]
        mins = [jnp.min(jnp.minimum(c[0], c[1])) for c in carries]
        accs = [st[1] for st in states]
        for u, g in work:
            atts = []
            for h in range(2):
                sl = slice(h * n, (h + 1) * n)
                att = jnp.exp2(z2s[u, g][:, sl] - sp2s[u, g][:, sl] - laters[u, g, h])
                if diagonal and g == 0:
                    att = jnp.where(before, att, 0.0)
                atts.append(att.astype(BF16))
            pv = _dot(jnp.concatenate(atts, axis=1), vs_ref[jcs[u, g]])
            accs[u] = pv if accs[u] is None else accs[u] + pv
        return [(carries[u], accs[u], mins[u]) for u in range(len(us))]

    us = list(range(ATTN_TILES))
    first = sweep(us, [0] * ATTN_TILES, [([None, None], None)] * ATTN_TILES, True)
    for u in us:
        carries, acc, min_carry = first[u]
        ngroups = (tiles[u] + ATTN_GROUP) // ATTN_GROUP

        def more(state, ngroups=ngroups):
            return jnp.logical_and(state[0] < ngroups, state[1] <= ATTN_STOP)

        def group(state, u=u):
            t, _, c0, c1, acc = state
            (carries, acc, min_carry), = sweep([u], [t], [([c0, c1], acc)], False)
            return t + 1, min_carry, carries[0], carries[1], acc

        state = lax.while_loop(more, group, (jnp.int32(1), min_carry, carries[0], carries[1], acc))
        out_ref[u * n:(u + 1) * n, :] = state[4].astype(BF16)


def _attention(qkv, bsz, seq):
    t = qkv.shape[0]
    n = SB_BLOCK
    nq = seq // n
    pairs = SB_HEADS // 2
    r = jnp.arange(n)
    tri = (r[:, None] > r[None, :]).astype(BF16)
    tt = jnp.concatenate([tri, jnp.ones((n, n), BF16)], axis=1)
    return pl.pallas_call(
        _attn_kernel,
        grid=(bsz, pairs, nq // ATTN_TILES),
        in_specs=[pl.BlockSpec((ATTN_TILES * n, LANES), lambda b, p, i: (b * (nq // ATTN_TILES) + i, p)),
                  pl.BlockSpec((seq, LANES), lambda b, p, i: (b, pairs + p)),
                  pl.BlockSpec((seq, LANES), lambda b, p, i: (b, 2 * pairs + p)),
                  _resident((n, 2 * n))],
        out_specs=pl.BlockSpec((ATTN_TILES * n, LANES), lambda b, p, i: (b * (nq // ATTN_TILES) + i, p)),
        out_shape=jax.ShapeDtypeStruct((t, SB_HEADS * SB_HEAD_DIM), BF16),
        scratch_shapes=[pltpu.VMEM((nq + 1, 2 * n, LANES), BF16), pltpu.VMEM((nq + 1, 2 * n, LANES), BF16)],
        compiler_params=pltpu.CompilerParams(dimension_semantics=("arbitrary", "arbitrary", "arbitrary"),
                                             vmem_limit_bytes=VMEM_LIMIT),
        name="stick_breaking_attention",
    )(qkv, qkv, qkv, tt)


def kernel(x, ssm_norm_w, ssm_in_w, ssm_conv_w, ssm_conv_b, ssm_dt_bias, ssm_a_log, ssm_d, ssm_gate_norm_w, ssm_out_w, kv_norm_w, w_k, w_v, attn_norm_w, w_q, w_o, ffn_norm_w, ffn_up_w, ffn_conv_w, ffn_conv_b, ffn_down_w, final_norm_w):
    bsz, seq, d = x.shape
    h = x.reshape(bsz * seq, d)
    zx, dt = _in_proj(h, seq, ssm_norm_w[0], ssm_in_w[0], ssm_conv_w[0], ssm_conv_b[0], ssm_dt_bias[0])
    y = _ssd(zx, dt, bsz, seq, ssm_a_log[0], ssm_d[0], ssm_gate_norm_w[0])
    h = _proj_ffn(y, ssm_out_w[0], h, seq, ffn_norm_w[0], ffn_up_w[0], ffn_conv_w[0], ffn_conv_b[0], ffn_down_w[0])
    qkv = _qkv_proj(h, seq, attn_norm_w[0], kv_norm_w, w_q[0], w_k, w_v)
    o = _attention(qkv, bsz, seq)
    h = _proj_ffn(o, w_o[0], h, seq, ffn_norm_w[1], ffn_up_w[1], ffn_conv_w[1], ffn_conv_b[1], ffn_down_w[1],
                  final_norm_w)
    return h.reshape(bsz, seq, d)
```

```python
import functools
import math

import jax
import jax.numpy as jnp
from jax import lax
from jax.experimental import pallas as pl
from jax.experimental.pallas import tpu as pltpu

D_MODEL = 1024
D_INNER = 2048
SSM_HEADS = 32
SSM_HEAD_DIM = 64
SSM_GROUPS = 4
SSM_STATE = 128
SSM_CONV = 4
SSM_CHUNK = 128
GN = SSM_GROUPS * SSM_STATE
CONV_DIM = D_INNER + 2 * GN
SB_HEADS = 16
SB_HEAD_DIM = 64
SB_BLOCK = 128
D_FF = 2816
FFN_CONV = 3
EPS = 1e-6

LANES = 128
HALO = 16
ROW_TILE = 512
COL_CHUNK = 512
FFN_CHUNK = 256
VMEM_LIMIT = 56 * 1024 * 1024
NEG_BIG = -1e30
LOG2E = math.log2(math.e)
CONV_ROWS = 128
CONV_AHEAD = 2
SSD_SUB = 4
ATTN_GROUP = 3
ATTN_TILES = 16
Q_SCALE = LOG2E / math.sqrt(SB_HEAD_DIM)
ATTN_STOP = 104.0 * LOG2E

F32 = jnp.float32
BF16 = jnp.bfloat16


def _rms(x, w):
    ms = jnp.mean(x * x, axis=-1, keepdims=True)
    return x * lax.rsqrt(ms + EPS) * w


def _softplus(x):
    return jnp.maximum(x, 0.0) + jnp.log(1.0 + jnp.exp(-jnp.abs(x)))


def _softplus_log2(x):
    return jnp.maximum(x, 0.0) + jnp.log(1.0 + jnp.exp2(-jnp.abs(x))) * LOG2E


def _silu(x):
    return x / (1.0 + jnp.exp(-x))


def _dot(a, b):
    return jnp.dot(a, b, preferred_element_type=F32)


def _dot_nt(a, b):
    return lax.dot_general(a, b, (((1,), (1,)), ((), ())), preferred_element_type=F32)


def _split3(x):
    h1 = x.astype(BF16)
    r1 = x - h1.astype(F32)
    h2 = r1.astype(BF16)
    h3 = (r1 - h2.astype(F32)).astype(BF16)
    return h1, h2, h3


def _resident(shape):
    return pl.BlockSpec(shape, lambda *_: (0,) * len(shape), pipeline_mode=pl.Buffered(1))


def _fill_normed(xn_ref, x_ref, xh_ref, nw, first):
    xn_ref[HALO:, :] = _rms(x_ref[...], nw).astype(BF16)
    xh = _rms(xh_ref[...], nw)
    xn_ref[:HALO, :] = jnp.where(first, 0.0, xh).astype(BF16)


def _store_panels(buf, first_panel, res):
    for p in range(res.shape[1] // LANES):
        buf[first_panel + p] = res[:, p * LANES:(p + 1) * LANES]


def _causal_conv(buf, panel, cw_ref, cb_ref, ch, taps, r0, rows):
    y = cb_ref[:, ch:ch + LANES]
    for k in range(taps):
        y = y + buf[panel, pl.ds(HALO - (taps - 1) + k + r0, rows), :] * cw_ref[k:k + 1, ch:ch + LANES]
    return y


def _in_proj_kernel(x_ref, xh_ref, nw_ref, w_ref, wdt_ref, cw_ref, cb_ref, dtb_ref,
                    out_ref, dt_ref, xn_ref, acc_ref, *, tm, tiles_per_seq):
    first = (pl.program_id(0) % tiles_per_seq) == 0
    _fill_normed(xn_ref, x_ref, xh_ref, nw_ref[...], first)
    xn = xn_ref[HALO:, :]
    dt_ref[...] = _softplus(_dot(xn, wdt_ref[...]) + dtb_ref[...])
    for c in range(D_INNER // COL_CHUNK):
        lo = c * COL_CHUNK
        out_ref[:, lo:lo + COL_CHUNK] = _silu(_dot(xn, w_ref[:, lo:lo + COL_CHUNK])).astype(BF16)
    nchunks = CONV_DIM // COL_CHUNK
    nbuf = acc_ref.shape[0]

    def xbc_proj(c):
        lo = D_INNER + c * COL_CHUNK
        _store_panels(acc_ref.at[c % nbuf], 0, _dot(xn_ref[...], w_ref[:, lo:lo + COL_CHUNK]))

    for c in range(min(CONV_AHEAD, nchunks)):
        xbc_proj(c)
    for c in range(nchunks):
        lo = c * COL_CHUNK
        if c + CONV_AHEAD < nchunks:
            xbc_proj(c + CONV_AHEAD)
        for p in range(COL_CHUNK // LANES):
            ch = lo + p * LANES
            for r0 in range(0, tm, CONV_ROWS):
                y = _causal_conv(acc_ref.at[c % nbuf], p, cw_ref, cb_ref, ch, SSM_CONV, r0, CONV_ROWS)
                out_ref[r0:r0 + CONV_ROWS, D_INNER + ch:D_INNER + ch + LANES] = _silu(y).astype(BF16)


def _row_specs(tm, d):
    per = tm // HALO
    return (pl.BlockSpec((tm, d), lambda i: (i, 0)),
            pl.BlockSpec((HALO, d), lambda i: (jnp.maximum(i * per - 1, 0), 0)))


def _in_proj(h, seq, norm_w, w_in, conv_w, conv_b, dt_bias):
    t, d = h.shape
    tm = min(ROW_TILE, seq)
    w_main = w_in[:, :D_INNER + CONV_DIM].astype(BF16)
    rep = LANES // SSM_HEADS
    w_dt = jnp.tile(w_in[:, D_INNER + CONV_DIM:], (1, rep)).astype(BF16)
    dtb = jnp.tile(dt_bias, rep).reshape(1, LANES)
    x_spec, xh_spec = _row_specs(tm, d)
    return pl.pallas_call(
        functools.partial(_in_proj_kernel, tm=tm, tiles_per_seq=seq // tm),
        grid=(t // tm,),
        in_specs=[x_spec, xh_spec, _resident((1, d)), _resident(w_main.shape), _resident(w_dt.shape),
                  _resident(conv_w.shape), _resident((1, CONV_DIM)), _resident((1, LANES))],
        out_specs=[pl.BlockSpec((tm, D_INNER + CONV_DIM), lambda i: (i, 0)),
                   pl.BlockSpec((tm, LANES), lambda i: (i, 0))],
        out_shape=[jax.ShapeDtypeStruct((t, D_INNER + CONV_DIM), BF16),
                   jax.ShapeDtypeStruct((t, LANES), F32)],
        scratch_shapes=[pltpu.VMEM((tm + HALO, d), BF16), pltpu.VMEM((CONV_AHEAD + 1, COL_CHUNK // LANES, tm + HALO, LANES), F32)],
        compiler_params=pltpu.CompilerParams(dimension_semantics=("arbitrary",), vmem_limit_bytes=VMEM_LIMIT),
        name="ssm_in_proj",
    )(h, h, norm_w.reshape(1, d), w_main, w_dt, conv_w, conv_b.reshape(1, CONV_DIM), dtb)


def _ssd_kernel(z_ref, xs_ref, b_ref, c_ref, dt_ref, alog_ref, dskip_ref, gnw_ref, ee_ref, ew_ref,
                y_ref, state_ref, yg_ref, e_ref, w_ref):
    @pl.when(pl.program_id(1) == 0)
    def _():
        state_ref[...] = jnp.zeros_like(state_ref)

    n = SSM_CHUNK
    row = lax.broadcasted_iota(jnp.int32, (n, n), 0)
    col = lax.broadcasted_iota(jnp.int32, (n, n), 1)
    causal = col <= row
    low_half = col < SSM_HEAD_DIM
    tri = causal.astype(BF16)

    def hi_lo(x):
        hi = x.astype(BF16)
        return hi, (x - hi.astype(F32)).astype(BF16)

    def decays(u):
        dt = dt_ref[u * n:(u + 1) * n, :]
        ad = dt * (-jnp.exp(alog_ref[...]) * LOG2E)
        a1, a2, a3 = _split3(ad)
        acs = _dot(tri, a1) + _dot(tri, a2) + _dot(tri, a3)
        acs_last = acs[n - 1:n, :]
        src_t = (acs - jnp.log(dt) * LOG2E).T
        e_hi, e_lo = hi_lo(jnp.exp2(acs))
        w_hi, w_lo = hi_lo(jnp.exp2(acs_last - acs) * dt)
        q = SSM_HEADS
        packed = jnp.where(col < q, e_hi, jnp.where(col < 2 * q, e_lo, jnp.where(col < 3 * q, w_hi, w_lo)))
        for c in range(D_INNER // COL_CHUNK):
            lo = c * COL_CHUNK
            e_ref[u, :, lo:lo + COL_CHUNK] = _dot(packed, ee_ref[:, lo:lo + COL_CHUNK])
            w_ref[u, :, lo:lo + COL_CHUNK] = _dot(packed, ew_ref[:, lo:lo + COL_CHUNK])
        return acs, src_t

    terms = [decays(u) for u in range(SSD_SUB)]
    for u in range(SSD_SUB):
        acs, src_t = terms[u]
        rows = slice(u * n, (u + 1) * n)
        for g in range(SSM_GROUPS):
            gl = g * SSM_STATE
            bg = b_ref[rows, gl:gl + SSM_STATE]
            cg = c_ref[rows, gl:gl + SSM_STATE]
            cb = _dot_nt(cg, bg)
            bg_t = bg.astype(F32).T.astype(BF16)
            for p in range(SSM_HEADS // SSM_GROUPS // 2):
                h0 = g * (SSM_HEADS // SSM_GROUPS) + 2 * p
                lo = h0 * SSM_HEAD_DIM
                xs_pair = xs_ref[rows, lo:lo + LANES]
                xs_f = xs_pair.astype(F32)
                ms = []
                for h in (h0, h0 + 1):
                    diff = acs[:, h:h + 1] - src_t[h:h + 1, :]
                    ms.append((cb * jnp.exp2(jnp.where(causal, diff, NEG_BIG))).astype(BF16))
                zero = jnp.zeros_like(xs_pair)
                rhs = jnp.concatenate([jnp.where(low_half, xs_pair, zero), jnp.where(low_half, zero, xs_pair)], axis=0)
                y_diag = _dot(jnp.concatenate(ms, axis=1), rhs)
                s_pair = state_ref[:, lo:lo + LANES]
                y_off = _dot(cg, s_pair.astype(BF16)) * e_ref[u, :, lo:lo + LANES]
                xw = (xs_f * w_ref[u, :, lo:lo + LANES]).astype(BF16)
                state_ref[:, lo:lo + LANES] = s_pair * e_ref[u, n - 1:n, lo:lo + LANES] + _dot(bg_t, xw)
                y = y_diag + y_off + dskip_ref[:, lo:lo + LANES] * xs_f
                yg_ref[u, :, lo:lo + LANES] = y * z_ref[rows, lo:lo + LANES].astype(F32)
            gw = D_INNER // SSM_GROUPS
            yg = yg_ref[u, :, g * gw:(g + 1) * gw]
            y_ref[rows, g * gw:(g + 1) * gw] = _rms(yg, gnw_ref[:, g * gw:(g + 1) * gw]).astype(BF16)


def _head_spread(first_group):
    k = jnp.arange(LANES)[:, None]
    head = jnp.arange(D_INNER)[None, :] // SSM_HEAD_DIM
    group = k // SSM_HEADS
    return ((k % SSM_HEADS == head) & (group >= first_group) & (group < first_group + 2)).astype(BF16)


def _ssd(zx, dt, bsz, seq, a_log, d_skip, gate_norm_w):
    t = zx.shape[0]
    rows = SSD_SUB * SSM_CHUNK
    nc = seq // rows
    alog = jnp.tile(a_log, LANES // SSM_HEADS).reshape(1, LANES)
    dskip = jnp.repeat(d_skip, SSM_HEAD_DIM).reshape(1, D_INNER)
    rowblk = lambda b, c: b * nc + c
    return pl.pallas_call(
        _ssd_kernel,
        grid=(bsz, nc),
        in_specs=[pl.BlockSpec((rows, D_INNER), lambda b, c: (rowblk(b, c), 0)),
                  pl.BlockSpec((rows, D_INNER), lambda b, c: (rowblk(b, c), 1)),
                  pl.BlockSpec((rows, GN), lambda b, c: (rowblk(b, c), 2 * D_INNER // GN)),
                  pl.BlockSpec((rows, GN), lambda b, c: (rowblk(b, c), 2 * D_INNER // GN + 1)),
                  pl.BlockSpec((rows, LANES), lambda b, c: (rowblk(b, c), 0)),
                  _resident((1, LANES)), _resident((1, D_INNER)), _resident((1, D_INNER)),
                  _resident((LANES, D_INNER)), _resident((LANES, D_INNER))],
        out_specs=pl.BlockSpec((rows, D_INNER), lambda b, c: (rowblk(b, c), 0)),
        out_shape=jax.ShapeDtypeStruct((t, D_INNER), BF16),
        scratch_shapes=[pltpu.VMEM((SSM_STATE, D_INNER), F32)] + [pltpu.VMEM((SSD_SUB, SSM_CHUNK, D_INNER), F32)] * 3,
        compiler_params=pltpu.CompilerParams(dimension_semantics=("arbitrary", "arbitrary"),
                                             vmem_limit_bytes=VMEM_LIMIT),
        name="ssd_scan",
    )(zx, zx, zx, zx, dt, alog, dskip, gate_norm_w.reshape(1, D_INNER), _head_spread(0), _head_spread(2))


def _ffn_kernel(a_ref, wp_ref, res_ref, nw_ref, wup_ref, cw_ref, cb_ref, wd_ref, fnw_ref,
                out_ref, xn_ref, tail_ref, h_ref, hid_ref, *, tm, tiles_per_seq, final_norm):
    first = (pl.program_id(0) % tiles_per_seq) == 0
    for c in range(D_MODEL // COL_CHUNK):
        lo = c * COL_CHUNK
        out_ref[:, lo:lo + COL_CHUNK] = res_ref[:, lo:lo + COL_CHUNK] + _dot(a_ref[...], wp_ref[:, lo:lo + COL_CHUNK])

    @pl.when(first)
    def _():
        tail_ref[...] = jnp.zeros_like(tail_ref)

    xn_ref[:HALO, :] = tail_ref[...]
    xn_ref[HALO:, :] = _rms(out_ref[...], nw_ref[...]).astype(BF16)
    tail_ref[...] = xn_ref[tm:, :]
    fc = FFN_CHUNK
    nchunks = D_FF // fc

    nbuf = h_ref.shape[0]

    def up_proj(c):
        buf = h_ref.at[c % nbuf]
        _store_panels(buf, 0, _dot(xn_ref[...], wup_ref[:, c * fc:(c + 1) * fc]))
        _store_panels(buf, fc // LANES, _dot(xn_ref[...], wup_ref[:, D_FF + c * fc:D_FF + (c + 1) * fc]))

    for c in range(min(CONV_AHEAD, nchunks)):
        up_proj(c)
    for c in range(nchunks):
        if c + CONV_AHEAD < nchunks:
            up_proj(c + CONV_AHEAD)
        buf = h_ref.at[c % nbuf]
        for p in range(fc // LANES):
            ch = c * fc + p * LANES
            for r0 in range(0, tm, CONV_ROWS):
                gate = _causal_conv(buf, p, cw_ref, cb_ref, ch, FFN_CONV, r0, CONV_ROWS)
                value = _causal_conv(buf, fc // LANES + p, cw_ref, cb_ref, D_FF + ch, FFN_CONV, r0, CONV_ROWS)
                hid_ref[r0:r0 + CONV_ROWS, ch:ch + LANES] = (_silu(gate) * value).astype(BF16)
    for c in range(D_MODEL // COL_CHUNK):
        lo = c * COL_CHUNK
        out_ref[:, lo:lo + COL_CHUNK] += _dot(hid_ref[...], wd_ref[:, lo:lo + COL_CHUNK])
    if final_norm:
        out_ref[...] = _rms(out_ref[...], fnw_ref[...])


def _proj_ffn(a, w_proj, res, seq, norm_w, w_up, conv_w, conv_b, w_down, final_norm_w=None):
    t, k = a.shape
    d = res.shape[1]
    tm = min(ROW_TILE, seq)
    final_norm = final_norm_w is not None
    fnw = (final_norm_w if final_norm else jnp.ones((d,), F32)).reshape(1, d)
    return pl.pallas_call(
        functools.partial(_ffn_kernel, tm=tm, tiles_per_seq=seq // tm, final_norm=final_norm),
        grid=(t // tm,),
        in_specs=[pl.BlockSpec((tm, k), lambda i: (i, 0)), _resident((k, d)), pl.BlockSpec((tm, d), lambda i: (i, 0)),
                  _resident((1, d)), _resident((d, 2 * D_FF)), _resident((FFN_CONV, 2 * D_FF)),
                  _resident((1, 2 * D_FF)), _resident((D_FF, d)), _resident((1, d))],
        out_specs=pl.BlockSpec((tm, d), lambda i: (i, 0)),
        out_shape=jax.ShapeDtypeStruct((t, d), F32),
        scratch_shapes=[pltpu.VMEM((tm + HALO, d), BF16), pltpu.VMEM((HALO, d), BF16),
                        pltpu.VMEM((CONV_AHEAD + 1, 2 * FFN_CHUNK // LANES, tm + HALO, LANES), F32),
                        pltpu.VMEM((tm, D_FF), BF16)],
        compiler_params=pltpu.CompilerParams(dimension_semantics=("arbitrary",), vmem_limit_bytes=VMEM_LIMIT),
        name="proj_conv_ffn",
    )(a, w_proj.astype(BF16), res, norm_w.reshape(1, d), w_up.astype(BF16), conv_w,
      conv_b.reshape(1, 2 * D_FF), w_down.astype(BF16), fnw)


def _qkv_kernel(x_ref, qnw_ref, kvnw_ref, w_ref, out_ref):
    x = x_ref[...]
    xhat = x * lax.rsqrt(jnp.mean(x * x, axis=-1, keepdims=True) + EPS)
    xq = (xhat * qnw_ref[...]).astype(BF16)
    xkv = (xhat * kvnw_ref[...]).astype(BF16)
    for c in range(3 * D_MODEL // COL_CHUNK):
        lo = c * COL_CHUNK
        if lo < D_MODEL:
            out_ref[:, lo:lo + COL_CHUNK] = (_dot(xq, w_ref[:, lo:lo + COL_CHUNK]) * Q_SCALE).astype(BF16)
        else:
            out_ref[:, lo:lo + COL_CHUNK] = _dot(xkv, w_ref[:, lo:lo + COL_CHUNK]).astype(BF16)


def _qkv_proj(h, seq, q_norm_w, kv_norm_w, w_q, w_k, w_v):
    t, d = h.shape
    tm = min(ROW_TILE, seq)
    w = jnp.concatenate([w_q, w_k, w_v], axis=1).astype(BF16)
    return pl.pallas_call(
        _qkv_kernel,
        grid=(t // tm,),
        in_specs=[pl.BlockSpec((tm, d), lambda i: (i, 0)), _resident((1, d)), _resident((1, d)),
                  _resident((d, 3 * d))],
        out_specs=pl.BlockSpec((tm, 3 * d), lambda i: (i, 0)),
        out_shape=jax.ShapeDtypeStruct((t, 3 * d), BF16),
        compiler_params=pltpu.CompilerParams(dimension_semantics=("arbitrary",), vmem_limit_bytes=VMEM_LIMIT),
        name="qkv_proj",
    )(h, q_norm_w.reshape(1, d), kv_norm_w.reshape(1, d), w)


def _attn_kernel(q_ref, k_ref, v_ref, tt_ref, out_ref, ks_ref, vs_ref):
    i = pl.program_id(2)
    n = SB_BLOCK
    nblk = ks_ref.shape[0] - 1
    row = lax.broadcasted_iota(jnp.int32, (n, n), 0)
    col = lax.broadcasted_iota(jnp.int32, (n, n), 1)
    low_half = col < SB_HEAD_DIM
    before = col < row

    @pl.when(i == 0)
    def _():
        def fill(j, _):
            sl = pl.ds(pl.multiple_of(j * n, n), n)
            kj = k_ref[sl, :]
            vj = v_ref[sl, :]
            zero = jnp.zeros_like(kj)
            ks_ref[j] = jnp.concatenate([jnp.where(low_half, kj, zero), jnp.where(low_half, zero, kj)], axis=0)
            vs_ref[j] = jnp.concatenate([jnp.where(low_half, vj, zero), jnp.where(low_half, zero, vj)], axis=0)
            return 0
        lax.fori_loop(0, nblk, fill, 0)
        ks_ref[nblk] = jnp.zeros(ks_ref.shape[1:], BF16)
        vs_ref[nblk] = jnp.zeros(vs_ref.shape[1:], BF16)

    tt = tt_ref[...]
    tiles = [i * ATTN_TILES + u for u in range(ATTN_TILES)]
    qss = [q_ref[u * n:(u + 1) * n, :] for u in range(ATTN_TILES)]

    def sweep(us, ts, states, diagonal):
        work = [(u, g) for u in range(len(us)) for g in range(ATTN_GROUP)]
        jcs, z2s, sp2s, laters = {}, {}, {}, {}
        for u, g in work:
            j = tiles[us[u]] - (ts[u] * ATTN_GROUP + g)
            jcs[u, g] = jnp.where(j >= 0, j, nblk)
            z2s[u, g] = _dot_nt(qss[us[u]], ks_ref[jcs[u, g]])
        for key in work:
            sp2s[key] = _softplus_log2(z2s[key])
        carries = [list(st[0]) for st in states]
        for u, g in work:
            for h in range(2):
                sp = sp2s[u, g][:, h * n:(h + 1) * n]
                if diagonal and g == 0:
                    sp = jnp.where(before, sp, 0.0)
                cs = _dot(sp.astype(BF16), tt)
                if carries[u][h] is None:
                    laters[u, g, h], carries[u][h] = cs[:, :n], cs[:, n:]
                else:
                    laters[u, g, h] = cs[:, :n] + carries[u][h]
                    carries[u][h] = carries[u][h] + cs[:, n:]
        mins = [jnp.min(jnp.minimum(c[0], c[1])) for c in carries]
        accs = [st[1] for st in states]
        for u, g in work:
            atts = []
            for h in range(2):
                sl = slice(h * n, (h + 1) * n)
                att = jnp.exp2(z2s[u, g][:, sl] - sp2s[u, g][:, sl] - laters[u, g, h])
                if diagonal and g == 0:
                    att = jnp.where(before, att, 0.0)
                atts.append(att.astype(BF16))
            pv = _dot(jnp.concatenate(atts, axis=1), vs_ref[jcs[u, g]])
            accs[u] = pv if accs[u] is None else accs[u] + pv
        return [(carries[u], accs[u], mins[u]) for u in range(len(us))]

    us = list(range(ATTN_TILES))
    first = sweep(us, [0] * ATTN_TILES, [([None, None], None)] * ATTN_TILES, True)
    for u in us:
        carries, acc, min_carry = first[u]
        ngroups = (tiles[u] + ATTN_GROUP) // ATTN_GROUP

        def more(state, ngroups=ngroups):
            return jnp.logical_and(state[0] < ngroups, state[1] <= ATTN_STOP)

        def group(state, u=u):
            t, _, c0, c1, acc = state
            (carries, acc, min_carry), = sweep([u], [t], [([c0, c1], acc)], False)
            return t + 1, min_carry, carries[0], carries[1], acc

        state = lax.while_loop(more, group, (jnp.int32(1), min_carry, carries[0], carries[1], acc))
        out_ref[u * n:(u + 1) * n, :] = state[4].astype(BF16)


def _attention(qkv, bsz, seq):
    t = qkv.shape[0]
    n = SB_BLOCK
    nq = seq // n
    pairs = SB_HEADS // 2
    r = jnp.arange(n)
    tri = (r[:, None] > r[None, :]).astype(BF16)
    tt = jnp.concatenate([tri, jnp.ones((n, n), BF16)], axis=1)
    return pl.pallas_call(
        _attn_kernel,
        grid=(bsz, pairs, nq // ATTN_TILES),
        in_specs=[pl.BlockSpec((ATTN_TILES * n, LANES), lambda b, p, i: (b * (nq // ATTN_TILES) + i, p)),
                  pl.BlockSpec((seq, LANES), lambda b, p, i: (b, pairs + p)),
                  pl.BlockSpec((seq, LANES), lambda b, p, i: (b, 2 * pairs + p)),
                  _resident((n, 2 * n))],
        out_specs=pl.BlockSpec((ATTN_TILES * n, LANES), lambda b, p, i: (b * (nq // ATTN_TILES) + i, p)),
        out_shape=jax.ShapeDtypeStruct((t, SB_HEADS * SB_HEAD_DIM), BF16),
        scratch_shapes=[pltpu.VMEM((nq + 1, 2 * n, LANES), BF16), pltpu.VMEM((nq + 1, 2 * n, LANES), BF16)],
        compiler_params=pltpu.CompilerParams(dimension_semantics=("arbitrary", "arbitrary", "arbitrary"),
                                             vmem_limit_bytes=VMEM_LIMIT),
        name="stick_breaking_attention",
    )(qkv, qkv, qkv, tt)


def kernel(x, ssm_norm_w, ssm_in_w, ssm_conv_w, ssm_conv_b, ssm_dt_bias, ssm_a_log, ssm_d, ssm_gate_norm_w, ssm_out_w, kv_norm_w, w_k, w_v, attn_norm_w, w_q, w_o, ffn_norm_w, ffn_up_w, ffn_conv_w, ffn_conv_b, ffn_down_w, final_norm_w):
    bsz, seq, d = x.shape
    h = x.reshape(bsz * seq, d)
    zx, dt = _in_proj(h, seq, ssm_norm_w[0], ssm_in_w[0], ssm_conv_w[0], ssm_conv_b[0], ssm_dt_bias[0])
    y = _ssd(zx, dt, bsz, seq, ssm_a_log[0], ssm_d[0], ssm_gate_norm_w[0])
    h = _proj_ffn(y, ssm_out_w[0], h, seq, ffn_norm_w[0], ffn_up_w[0], ffn_conv_w[0], ffn_conv_b[0], ffn_down_w[0])
    qkv = _qkv_proj(h, seq, attn_norm_w[0], kv_norm_w, w_q[0], w_k, w_v)
    o = _attention(qkv, bsz, seq)
    h = _proj_ffn(o, w_o[0], h, seq, ffn_norm_w[1], ffn_up_w[1], ffn_conv_w[1], ffn_conv_b[1], ffn_down_w[1],
                  final_norm_w)
    return h.reshape(bsz, seq, d)
```

```python
import functools
import math

import jax
import jax.numpy as jnp
from jax import lax
from jax.experimental import pallas as pl
from jax.experimental.pallas import tpu as pltpu

D_MODEL = 1024
D_INNER = 2048
SSM_HEADS = 32
SSM_HEAD_DIM = 64
SSM_GROUPS = 4
SSM_STATE = 128
SSM_CONV = 4
SSM_CHUNK = 128
GN = SSM_GROUPS * SSM_STATE
CONV_DIM = D_INNER + 2 * GN
SB_HEADS = 16
SB_HEAD_DIM = 64
SB_BLOCK = 128
D_FF = 2816
FFN_CONV = 3
EPS = 1e-6

LANES = 128
HALO = 16
ROW_TILE = 512
COL_CHUNK = 512
FFN_CHUNK = 256
VMEM_LIMIT = 56 * 1024 * 1024
NEG_BIG = -1e30
LOG2E = math.log2(math.e)
CONV_ROWS = 128
CONV_AHEAD = 2
SSD_SUB = 4
ATTN_GROUP = 3
ATTN_TILES = 32
Q_SCALE = LOG2E / math.sqrt(SB_HEAD_DIM)
ATTN_STOP = 104.0 * LOG2E

F32 = jnp.float32
BF16 = jnp.bfloat16


def _rms(x, w):
    ms = jnp.mean(x * x, axis=-1, keepdims=True)
    return x * lax.rsqrt(ms + EPS) * w


def _softplus(x):
    return jnp.maximum(x, 0.0) + jnp.log(1.0 + jnp.exp(-jnp.abs(x)))


def _softplus_log2(x):
    return jnp.maximum(x, 0.0) + jnp.log(1.0 + jnp.exp2(-jnp.abs(x))) * LOG2E


def _silu(x):
    return x / (1.0 + jnp.exp(-x))


def _dot(a, b):
    return jnp.dot(a, b, preferred_element_type=F32)


def _dot_nt(a, b):
    return lax.dot_general(a, b, (((1,), (1,)), ((), ())), preferred_element_type=F32)


def _split3(x):
    h1 = x.astype(BF16)
    r1 = x - h1.astype(F32)
    h2 = r1.astype(BF16)
    h3 = (r1 - h2.astype(F32)).astype(BF16)
    return h1, h2, h3


def _resident(shape):
    return pl.BlockSpec(shape, lambda *_: (0,) * len(shape), pipeline_mode=pl.Buffered(1))


def _fill_normed(xn_ref, x_ref, xh_ref, nw, first):
    xn_ref[HALO:, :] = _rms(x_ref[...], nw).astype(BF16)
    xh = _rms(xh_ref[...], nw)
    xn_ref[:HALO, :] = jnp.where(first, 0.0, xh).astype(BF16)


def _store_panels(buf, first_panel, res):
    for p in range(res.shape[1] // LANES):
        buf[first_panel + p] = res[:, p * LANES:(p + 1) * LANES]


def _causal_conv(buf, panel, cw_ref, cb_ref, ch, taps, r0, rows):
    y = cb_ref[:, ch:ch + LANES]
    for k in range(taps):
        y = y + buf[panel, pl.ds(HALO - (taps - 1) + k + r0, rows), :] * cw_ref[k:k + 1, ch:ch + LANES]
    return y


def _in_proj_kernel(x_ref, xh_ref, nw_ref, w_ref, wdt_ref, cw_ref, cb_ref, dtb_ref,
                    out_ref, dt_ref, xn_ref, acc_ref, *, tm, tiles_per_seq):
    first = (pl.program_id(0) % tiles_per_seq) == 0
    _fill_normed(xn_ref, x_ref, xh_ref, nw_ref[...], first)
    xn = xn_ref[HALO:, :]
    dt_ref[...] = _softplus(_dot(xn, wdt_ref[...]) + dtb_ref[...])
    for c in range(D_INNER // COL_CHUNK):
        lo = c * COL_CHUNK
        out_ref[:, lo:lo + COL_CHUNK] = _silu(_dot(xn, w_ref[:, lo:lo + COL_CHUNK])).astype(BF16)
    nchunks = CONV_DIM // COL_CHUNK
    nbuf = acc_ref.shape[0]

    def xbc_proj(c):
        lo = D_INNER + c * COL_CHUNK
        _store_panels(acc_ref.at[c % nbuf], 0, _dot(xn_ref[...], w_ref[:, lo:lo + COL_CHUNK]))

    for c in range(min(CONV_AHEAD, nchunks)):
        xbc_proj(c)
    for c in range(nchunks):
        lo = c * COL_CHUNK
        if c + CONV_AHEAD < nchunks:
            xbc_proj(c + CONV_AHEAD)
        for p in range(COL_CHUNK // LANES):
            ch = lo + p * LANES
            for r0 in range(0, tm, CONV_ROWS):
                y = _causal_conv(acc_ref.at[c % nbuf], p, cw_ref, cb_ref, ch, SSM_CONV, r0, CONV_ROWS)
                out_ref[r0:r0 + CONV_ROWS, D_INNER + ch:D_INNER + ch + LANES] = _silu(y).astype(BF16)


def _row_specs(tm, d):
    per = tm // HALO
    return (pl.BlockSpec((tm, d), lambda i: (i, 0)),
            pl.BlockSpec((HALO, d), lambda i: (jnp.maximum(i * per - 1, 0), 0)))


def _in_proj(h, seq, norm_w, w_in, conv_w, conv_b, dt_bias):
    t, d = h.shape
    tm = min(ROW_TILE, seq)
    w_main = w_in[:, :D_INNER + CONV_DIM].astype(BF16)
    rep = LANES // SSM_HEADS
    w_dt = jnp.tile(w_in[:, D_INNER + CONV_DIM:], (1, rep)).astype(BF16)
    dtb = jnp.tile(dt_bias, rep).reshape(1, LANES)
    x_spec, xh_spec = _row_specs(tm, d)
    return pl.pallas_call(
        functools.partial(_in_proj_kernel, tm=tm, tiles_per_seq=seq // tm),
        grid=(t // tm,),
        in_specs=[x_spec, xh_spec, _resident((1, d)), _resident(w_main.shape), _resident(w_dt.shape),
                  _resident(conv_w.shape), _resident((1, CONV_DIM)), _resident((1, LANES))],
        out_specs=[pl.BlockSpec((tm, D_INNER + CONV_DIM), lambda i: (i, 0)),
                   pl.BlockSpec((tm, LANES), lambda i: (i, 0))],
        out_shape=[jax.ShapeDtypeStruct((t, D_INNER + CONV_DIM), BF16),
                   jax.ShapeDtypeStruct((t, LANES), F32)],
        scratch_shapes=[pltpu.VMEM((tm + HALO, d), BF16), pltpu.VMEM((CONV_AHEAD + 1, COL_CHUNK // LANES, tm + HALO, LANES), F32)],
        compiler_params=pltpu.CompilerParams(dimension_semantics=("arbitrary",), vmem_limit_bytes=VMEM_LIMIT),
        name="ssm_in_proj",
    )(h, h, norm_w.reshape(1, d), w_main, w_dt, conv_w, conv_b.reshape(1, CONV_DIM), dtb)


def _ssd_kernel(z_ref, xs_ref, b_ref, c_ref, dt_ref, alog_ref, dskip_ref, gnw_ref, ee_ref, ew_ref,
                y_ref, state_ref, yg_ref, e_ref, w_ref):
    @pl.when(pl.program_id(1) == 0)
    def _():
        state_ref[...] = jnp.zeros_like(state_ref)

    n = SSM_CHUNK
    row = lax.broadcasted_iota(jnp.int32, (n, n), 0)
    col = lax.broadcasted_iota(jnp.int32, (n, n), 1)
    causal = col <= row
    low_half = col < SSM_HEAD_DIM
    tri = causal.astype(BF16)

    def hi_lo(x):
        hi = x.astype(BF16)
        return hi, (x - hi.astype(F32)).astype(BF16)

    def decays(u):
        dt = dt_ref[u * n:(u + 1) * n, :]
        ad = dt * (-jnp.exp(alog_ref[...]) * LOG2E)
        a1, a2, a3 = _split3(ad)
        acs = _dot(tri, a1) + _dot(tri, a2) + _dot(tri, a3)
        acs_last = acs[n - 1:n, :]
        src_t = (acs - jnp.log(dt) * LOG2E).T
        e_hi, e_lo = hi_lo(jnp.exp2(acs))
        w_hi, w_lo = hi_lo(jnp.exp2(acs_last - acs) * dt)
        q = SSM_HEADS
        packed = jnp.where(col < q, e_hi, jnp.where(col < 2 * q, e_lo, jnp.where(col < 3 * q, w_hi, w_lo)))
        for c in range(D_INNER // COL_CHUNK):
            lo = c * COL_CHUNK
            e_ref[u, :, lo:lo + COL_CHUNK] = _dot(packed, ee_ref[:, lo:lo + COL_CHUNK])
            w_ref[u, :, lo:lo + COL_CHUNK] = _dot(packed, ew_ref[:, lo:lo + COL_CHUNK])
        return acs, src_t

    terms = [decays(u) for u in range(SSD_SUB)]
    for u in range(SSD_SUB):
        acs, src_t = terms[u]
        rows = slice(u * n, (u + 1) * n)
        for g in range(SSM_GROUPS):
            gl = g * SSM_STATE
            bg = b_ref[rows, gl:gl + SSM_STATE]
            cg = c_ref[rows, gl:gl + SSM_STATE]
            cb = _dot_nt(cg, bg)
            bg_t = bg.astype(F32).T.astype(BF16)
            for p in range(SSM_HEADS // SSM_GROUPS // 2):
                h0 = g * (SSM_HEADS // SSM_GROUPS) + 2 * p
                lo = h0 * SSM_HEAD_DIM
                xs_pair = xs_ref[rows, lo:lo + LANES]
                xs_f = xs_pair.astype(F32)
                ms = []
                for h in (h0, h0 + 1):
                    diff = acs[:, h:h + 1] - src_t[h:h + 1, :]
                    ms.append((cb * jnp.exp2(jnp.where(causal, diff, NEG_BIG))).astype(BF16))
                zero = jnp.zeros_like(xs_pair)
                rhs = jnp.concatenate([jnp.where(low_half, xs_pair, zero), jnp.where(low_half, zero, xs_pair)], axis=0)
                y_diag = _dot(jnp.concatenate(ms, axis=1), rhs)
                s_pair = state_ref[:, lo:lo + LANES]
                y_off = _dot(cg, s_pair.astype(BF16)) * e_ref[u, :, lo:lo + LANES]
                xw = (xs_f * w_ref[u, :, lo:lo + LANES]).astype(BF16)
                state_ref[:, lo:lo + LANES] = s_pair * e_ref[u, n - 1:n, lo:lo + LANES] + _dot(bg_t, xw)
                y = y_diag + y_off + dskip_ref[:, lo:lo + LANES] * xs_f
                yg_ref[u, :, lo:lo + LANES] = y * z_ref[rows, lo:lo + LANES].astype(F32)
            gw = D_INNER // SSM_GROUPS
            yg = yg_ref[u, :, g * gw:(g + 1) * gw]
            y_ref[rows, g * gw:(g + 1) * gw] = _rms(yg, gnw_ref[:, g * gw:(g + 1) * gw]).astype(BF16)


def _head_spread(first_group):
    k = jnp.arange(LANES)[:, None]
    head = jnp.arange(D_INNER)[None, :] // SSM_HEAD_DIM
    group = k // SSM_HEADS
    return ((k % SSM_HEADS == head) & (group >= first_group) & (group < first_group + 2)).astype(BF16)


def _ssd(zx, dt, bsz, seq, a_log, d_skip, gate_norm_w):
    t = zx.shape[0]
    rows = SSD_SUB * SSM_CHUNK
    nc = seq // rows
    alog = jnp.tile(a_log, LANES // SSM_HEADS).reshape(1, LANES)
    dskip = jnp.repeat(d_skip, SSM_HEAD_DIM).reshape(1, D_INNER)
    rowblk = lambda b, c: b * nc + c
    return pl.pallas_call(
        _ssd_kernel,
        grid=(bsz, nc),
        in_specs=[pl.BlockSpec((rows, D_INNER), lambda b, c: (rowblk(b, c), 0)),
                  pl.BlockSpec((rows, D_INNER), lambda b, c: (rowblk(b, c), 1)),
                  pl.BlockSpec((rows, GN), lambda b, c: (rowblk(b, c), 2 * D_INNER // GN)),
                  pl.BlockSpec((rows, GN), lambda b, c: (rowblk(b, c), 2 * D_INNER // GN + 1)),
                  pl.BlockSpec((rows, LANES), lambda b, c: (rowblk(b, c), 0)),
                  _resident((1, LANES)), _resident((1, D_INNER)), _resident((1, D_INNER)),
                  _resident((LANES, D_INNER)), _resident((LANES, D_INNER))],
        out_specs=pl.BlockSpec((rows, D_INNER), lambda b, c: (rowblk(b, c), 0)),
        out_shape=jax.ShapeDtypeStruct((t, D_INNER), BF16),
        scratch_shapes=[pltpu.VMEM((SSM_STATE, D_INNER), F32)] + [pltpu.VMEM((SSD_SUB, SSM_CHUNK, D_INNER), F32)] * 3,
        compiler_params=pltpu.CompilerParams(dimension_semantics=("arbitrary", "arbitrary"),
                                             vmem_limit_bytes=VMEM_LIMIT),
        name="ssd_scan",
    )(zx, zx, zx, zx, dt, alog, dskip, gate_norm_w.reshape(1, D_INNER), _head_spread(0), _head_spread(2))


def _ffn_kernel(a_ref, wp_ref, res_ref, nw_ref, wup_ref, cw_ref, cb_ref, wd_ref, fnw_ref,
                out_ref, xn_ref, tail_ref, h_ref, hid_ref, *, tm, tiles_per_seq, final_norm):
    first = (pl.program_id(0) % tiles_per_seq) == 0
    for c in range(D_MODEL // COL_CHUNK):
        lo = c * COL_CHUNK
        out_ref[:, lo:lo + COL_CHUNK] = res_ref[:, lo:lo + COL_CHUNK] + _dot(a_ref[...], wp_ref[:, lo:lo + COL_CHUNK])

    @pl.when(first)
    def _():
        tail_ref[...] = jnp.zeros_like(tail_ref)

    xn_ref[:HALO, :] = tail_ref[...]
    xn_ref[HALO:, :] = _rms(out_ref[...], nw_ref[...]).astype(BF16)
    tail_ref[...] = xn_ref[tm:, :]
    fc = FFN_CHUNK
    nchunks = D_FF // fc

    nbuf = h_ref.shape[0]

    def up_proj(c):
        buf = h_ref.at[c % nbuf]
        _store_panels(buf, 0, _dot(xn_ref[...], wup_ref[:, c * fc:(c + 1) * fc]))
        _store_panels(buf, fc // LANES, _dot(xn_ref[...], wup_ref[:, D_FF + c * fc:D_FF + (c + 1) * fc]))

    for c in range(min(CONV_AHEAD, nchunks)):
        up_proj(c)
    for c in range(nchunks):
        if c + CONV_AHEAD < nchunks:
            up_proj(c + CONV_AHEAD)
        buf = h_ref.at[c % nbuf]
        for p in range(fc // LANES):
            ch = c * fc + p * LANES
            for r0 in range(0, tm, CONV_ROWS):
                gate = _causal_conv(buf, p, cw_ref, cb_ref, ch, FFN_CONV, r0, CONV_ROWS)
                value = _causal_conv(buf, fc // LANES + p, cw_ref, cb_ref, D_FF + ch, FFN_CONV, r0, CONV_ROWS)
                hid_ref[r0:r0 + CONV_ROWS, ch:ch + LANES] = (_silu(gate) * value).astype(BF16)
    for c in range(D_MODEL // COL_CHUNK):
        lo = c * COL_CHUNK
        out_ref[:, lo:lo + COL_CHUNK] += _dot(hid_ref[...], wd_ref[:, lo:lo + COL_CHUNK])
    if final_norm:
        out_ref[...] = _rms(out_ref[...], fnw_ref[...])


def _proj_ffn(a, w_proj, res, seq, norm_w, w_up, conv_w, conv_b, w_down, final_norm_w=None):
    t, k = a.shape
    d = res.shape[1]
    tm = min(ROW_TILE, seq)
    final_norm = final_norm_w is not None
    fnw = (final_norm_w if final_norm else jnp.ones((d,), F32)).reshape(1, d)
    return pl.pallas_call(
        functools.partial(_ffn_kernel, tm=tm, tiles_per_seq=seq // tm, final_norm=final_norm),
        grid=(t // tm,),
        in_specs=[pl.BlockSpec((tm, k), lambda i: (i, 0)), _resident((k, d)), pl.BlockSpec((tm, d), lambda i: (i, 0)),
                  _resident((1, d)), _resident((d, 2 * D_FF)), _resident((FFN_CONV, 2 * D_FF)),
                  _resident((1, 2 * D_FF)), _resident((D_FF, d)), _resident((1, d))],
        out_specs=pl.BlockSpec((tm, d), lambda i: (i, 0)),
        out_shape=jax.ShapeDtypeStruct((t, d), F32),
        scratch_shapes=[pltpu.VMEM((tm + HALO, d), BF16), pltpu.VMEM((HALO, d), BF16),
                        pltpu.VMEM((CONV_AHEAD + 1, 2 * FFN_CHUNK // LANES, tm + HALO, LANES), F32),
                        pltpu.VMEM((tm, D_FF), BF16)],
        compiler_params=pltpu.CompilerParams(dimension_semantics=("arbitrary",), vmem_limit_bytes=VMEM_LIMIT),
        name="proj_conv_ffn",
    )(a, w_proj.astype(BF16), res, norm_w.reshape(1, d), w_up.astype(BF16), conv_w,
      conv_b.reshape(1, 2 * D_FF), w_down.astype(BF16), fnw)


def _qkv_kernel(x_ref, qnw_ref, kvnw_ref, w_ref, out_ref):
    x = x_ref[...]
    xhat = x * lax.rsqrt(jnp.mean(x * x, axis=-1, keepdims=True) + EPS)
    xq = (xhat * qnw_ref[...]).astype(BF16)
    xkv = (xhat * kvnw_ref[...]).astype(BF16)
    for c in range(3 * D_MODEL // COL_CHUNK):
        lo = c * COL_CHUNK
        if lo < D_MODEL:
            out_ref[:, lo:lo + COL_CHUNK] = (_dot(xq, w_ref[:, lo:lo + COL_CHUNK]) * Q_SCALE).astype(BF16)
        else:
            out_ref[:, lo:lo + COL_CHUNK] = _dot(xkv, w_ref[:, lo:lo + COL_CHUNK]).astype(BF16)


def _qkv_proj(h, seq, q_norm_w, kv_norm_w, w_q, w_k, w_v):
    t, d = h.shape
    tm = min(2 * ROW_TILE, seq)
    w = jnp.concatenate([w_q, w_k, w_v], axis=1).astype(BF16)
    return pl.pallas_call(
        _qkv_kernel,
        grid=(t // tm,),
        in_specs=[pl.BlockSpec((tm, d), lambda i: (i, 0)), _resident((1, d)), _resident((1, d)),
                  _resident((d, 3 * d))],
        out_specs=pl.BlockSpec((tm, 3 * d), lambda i: (i, 0)),
        out_shape=jax.ShapeDtypeStruct((t, 3 * d), BF16),
        compiler_params=pltpu.CompilerParams(dimension_semantics=("arbitrary",), vmem_limit_bytes=VMEM_LIMIT),
        name="qkv_proj",
    )(h, q_norm_w.reshape(1, d), kv_norm_w.reshape(1, d), w)


def _attn_kernel(q_ref, k_ref, v_ref, tt_ref, out_ref, ks_ref, vs_ref):
    i = pl.program_id(2)
    n = SB_BLOCK
    nblk = ks_ref.shape[0] - 1
    row = lax.broadcasted_iota(jnp.int32, (n, n), 0)
    col = lax.broadcasted_iota(jnp.int32, (n, n), 1)
    low_half = col < SB_HEAD_DIM
    before = col < row

    @pl.when(i == 0)
    def _():
        def fill(j, _):
            sl = pl.ds(pl.multiple_of(j * n, n), n)
            kj = k_ref[sl, :]
            vj = v_ref[sl, :]
            zero = jnp.zeros_like(kj)
            ks_ref[j] = jnp.concatenate([jnp.where(low_half, kj, zero), jnp.where(low_half, zero, kj)], axis=0)
            vs_ref[j] = jnp.concatenate([jnp.where(low_half, vj, zero), jnp.where(low_half, zero, vj)], axis=0)
            return 0
        lax.fori_loop(0, nblk, fill, 0)
        ks_ref[nblk] = jnp.zeros(ks_ref.shape[1:], BF16)
        vs_ref[nblk] = jnp.zeros(vs_ref.shape[1:], BF16)

    tt = tt_ref[...]
    tiles = [i * ATTN_TILES + u for u in range(ATTN_TILES)]
    qss = [q_ref[u * n:(u + 1) * n, :] for u in range(ATTN_TILES)]

    def sweep(us, ts, states, diagonal):
        work = [(u, g) for u in range(len(us)) for g in range(ATTN_GROUP)]
        jcs, z2s, sp2s, laters = {}, {}, {}, {}
        for u, g in work:
            j = tiles[us[u]] - (ts[u] * ATTN_GROUP + g)
            jcs[u, g] = jnp.where(j >= 0, j, nblk)
            z2s[u, g] = _dot_nt(qss[us[u]], ks_ref[jcs[u, g]])
        for key in work:
            sp2s[key] = _softplus_log2(z2s[key])
        carries = [list(st[0]) for st in states]
        for u, g in work:
            for h in range(2):
                sp = sp2s[u, g][:, h * n:(h + 1) * n]
                if diagonal and g == 0:
                    sp = jnp.where(before, sp, 0.0)
                cs = _dot(sp.astype(BF16), tt)
                if carries[u][h] is None:
                    laters[u, g, h], carries[u][h] = cs[:, :n], cs[:, n:]
                else:
                    laters[u, g, h] = cs[:, :n] + carries[u][h]
                    carries[u][h] = carries[u][h] + cs[:, n:]
        mins = [jnp.min(jnp.minimum(c[0], c[1])) for c in carries]
        accs = [st[1] for st in states]
        for u, g in work:
            atts = []
            for h in range(2):
                sl = slice(h * n, (h + 1) * n)
                att = jnp.exp2(z2s[u, g][:, sl] - sp2s[u, g][:, sl] - laters[u, g, h])
                if diagonal and g == 0:
                    att = jnp.where(before, att, 0.0)
                atts.append(att.astype(BF16))
            pv = _dot(jnp.concatenate(atts, axis=1), vs_ref[jcs[u, g]])
            accs[u] = pv if accs[u] is None else accs[u] + pv
        return [(carries[u], accs[u], mins[u]) for u in range(len(us))]

    us = list(range(ATTN_TILES))
    first = sweep(us, [0] * ATTN_TILES, [([None, None], None)] * ATTN_TILES, True)
    for u in us:
        carries, acc, min_carry = first[u]
        ngroups = (tiles[u] + ATTN_GROUP) // ATTN_GROUP

        def more(state, ngroups=ngroups):
            return jnp.logical_and(state[0] < ngroups, state[1] <= ATTN_STOP)

        def group(state, u=u):
            t, _, c0, c1, acc = state
            (carries, acc, min_carry), = sweep([u], [t], [([c0, c1], acc)], False)
            return t + 1, min_carry, carries[0], carries[1], acc

        state = lax.while_loop(more, group, (jnp.int32(1), min_carry, carries[0], carries[1], acc))
        out_ref[u * n:(u + 1) * n, :] = state[4].astype(BF16)


def _attention(qkv, bsz, seq):
    t = qkv.shape[0]
    n = SB_BLOCK
    nq = seq // n
    pairs = SB_HEADS // 2
    r = jnp.arange(n)
    tri = (r[:, None] > r[None, :]).astype(BF16)
    tt = jnp.concatenate([tri, jnp.ones((n, n), BF16)], axis=1)
    return pl.pallas_call(
        _attn_kernel,
        grid=(bsz, pairs, nq // ATTN_TILES),
        in_specs=[pl.BlockSpec((ATTN_TILES * n, LANES), lambda b, p, i: (b * (nq // ATTN_TILES) + i, p)),
                  pl.BlockSpec((seq, LANES), lambda b, p, i: (b, pairs + p)),
                  pl.BlockSpec((seq, LANES), lambda b, p, i: (b, 2 * pairs + p)),
                  _resident((n, 2 * n))],
        out_specs=pl.BlockSpec((ATTN_TILES * n, LANES), lambda b, p, i: (b * (nq // ATTN_TILES) + i, p)),
        out_shape=jax.ShapeDtypeStruct((t, SB_HEADS * SB_HEAD_DIM), BF16),
        scratch_shapes=[pltpu.VMEM((nq + 1, 2 * n, LANES), BF16), pltpu.VMEM((nq + 1, 2 * n, LANES), BF16)],
        compiler_params=pltpu.CompilerParams(dimension_semantics=("arbitrary", "arbitrary", "arbitrary"),
                                             vmem_limit_bytes=VMEM_LIMIT),
        name="stick_breaking_attention",
    )(qkv, qkv, qkv, tt)


def kernel(x, ssm_norm_w, ssm_in_w, ssm_conv_w, ssm_conv_b, ssm_dt_bias, ssm_a_log, ssm_d, ssm_gate_norm_w, ssm_out_w, kv_norm_w, w_k, w_v, attn_norm_w, w_q, w_o, ffn_norm_w, ffn_up_w, ffn_conv_w, ffn_conv_b, ffn_down_w, final_norm_w):
    bsz, seq, d = x.shape
    h = x.reshape(bsz * seq, d)
    zx, dt = _in_proj(h, seq, ssm_norm_w[0], ssm_in_w[0], ssm_conv_w[0], ssm_conv_b[0], ssm_dt_bias[0])
    y = _ssd(zx, dt, bsz, seq, ssm_a_log[0], ssm_d[0], ssm_gate_norm_w[0])
    h = _proj_ffn(y, ssm_out_w[0], h, seq, ffn_norm_w[0], ffn_up_w[0], ffn_conv_w[0], ffn_conv_b[0], ffn_down_w[0])
    qkv = _qkv_proj(h, seq, attn_norm_w[0], kv_norm_w, w_q[0], w_k, w_v)
    o = _attention(qkv, bsz, seq)
    h = _proj_ffn(o, w_o[0], h, seq, ffn_norm_w[1], ffn_up_w[1], ffn_conv_w[1], ffn_conv_b[1], ffn_down_w[1],
                  final_norm_w)
    return h.reshape(bsz, seq, d)
```

```python
import functools
import math

import jax
import jax.numpy as jnp
from jax import lax
from jax.experimental import pallas as pl
from jax.experimental.pallas import tpu as pltpu

D_MODEL = 1024
D_INNER = 2048
SSM_HEADS = 32
SSM_HEAD_DIM = 64
SSM_GROUPS = 4
SSM_STATE = 128
SSM_CONV = 4
SSM_CHUNK = 128
GN = SSM_GROUPS * SSM_STATE
CONV_DIM = D_INNER + 2 * GN
SB_HEADS = 16
SB_HEAD_DIM = 64
SB_BLOCK = 128
D_FF = 2816
FFN_CONV = 3
EPS = 1e-6

LANES = 128
HALO = 16
ROW_TILE = 512
COL_CHUNK = 512
FFN_CHUNK = 256
VMEM_LIMIT = 56 * 1024 * 1024
NEG_BIG = -1e30
LOG2E = math.log2(math.e)
CONV_ROWS = 128
CONV_AHEAD = 2
SSD_SUB = 4
ATTN_GROUP = 3
ATTN_TILES = 32
Q_SCALE = LOG2E / math.sqrt(SB_HEAD_DIM)
ATTN_STOP = 104.0 * LOG2E

F32 = jnp.float32
BF16 = jnp.bfloat16


def _rms(x, w):
    ms = jnp.mean(x * x, axis=-1, keepdims=True)
    return x * lax.rsqrt(ms + EPS) * w


def _softplus(x):
    return jnp.maximum(x, 0.0) + jnp.log(1.0 + jnp.exp(-jnp.abs(x)))


def _softplus_log2(x):
    return jnp.maximum(x, 0.0) + jnp.log(1.0 + jnp.exp2(-jnp.abs(x))) * LOG2E


def _silu(x):
    return x / (1.0 + jnp.exp(-x))


def _dot(a, b):
    return jnp.dot(a, b, preferred_element_type=F32)


def _dot_nt(a, b):
    return lax.dot_general(a, b, (((1,), (1,)), ((), ())), preferred_element_type=F32)


def _split3(x):
    h1 = x.astype(BF16)
    r1 = x - h1.astype(F32)
    h2 = r1.astype(BF16)
    h3 = (r1 - h2.astype(F32)).astype(BF16)
    return h1, h2, h3


def _resident(shape):
    return pl.BlockSpec(shape, lambda *_: (0,) * len(shape), pipeline_mode=pl.Buffered(1))


def _fill_normed(xn_ref, x_ref, xh_ref, nw, first):
    xn_ref[HALO:, :] = _rms(x_ref[...], nw).astype(BF16)
    xh = _rms(xh_ref[...], nw)
    xn_ref[:HALO, :] = jnp.where(first, 0.0, xh).astype(BF16)


def _store_panels(buf, first_panel, res):
    for p in range(res.shape[1] // LANES):
        buf[first_panel + p] = res[:, p * LANES:(p + 1) * LANES]


def _causal_conv(buf, panel, cw_ref, cb_ref, ch, taps, r0, rows):
    y = cb_ref[:, ch:ch + LANES]
    for k in range(taps):
        y = y + buf[panel, pl.ds(HALO - (taps - 1) + k + r0, rows), :] * cw_ref[k:k + 1, ch:ch + LANES]
    return y


def _in_proj_kernel(x_ref, xh_ref, nw_ref, w_ref, wdt_ref, cw_ref, cb_ref, dtb_ref,
                    out_ref, dt_ref, xn_ref, acc_ref, *, tm, tiles_per_seq):
    first = (pl.program_id(0) % tiles_per_seq) == 0
    _fill_normed(xn_ref, x_ref, xh_ref, nw_ref[...], first)
    xn = xn_ref[HALO:, :]
    dt_ref[...] = _softplus(_dot(xn, wdt_ref[...]) + dtb_ref[...])
    for c in range(D_INNER // COL_CHUNK):
        lo = c * COL_CHUNK
        out_ref[:, lo:lo + COL_CHUNK] = _silu(_dot(xn, w_ref[:, lo:lo + COL_CHUNK])).astype(BF16)
    nchunks = CONV_DIM // COL_CHUNK
    nbuf = acc_ref.shape[0]

    def xbc_proj(c):
        lo = D_INNER + c * COL_CHUNK
        _store_panels(acc_ref.at[c % nbuf], 0, _dot(xn_ref[...], w_ref[:, lo:lo + COL_CHUNK]))

    for c in range(min(CONV_AHEAD, nchunks)):
        xbc_proj(c)
    for c in range(nchunks):
        lo = c * COL_CHUNK
        if c + CONV_AHEAD < nchunks:
            xbc_proj(c + CONV_AHEAD)
        for p in range(COL_CHUNK // LANES):
            ch = lo + p * LANES
            for r0 in range(0, tm, CONV_ROWS):
                y = _causal_conv(acc_ref.at[c % nbuf], p, cw_ref, cb_ref, ch, SSM_CONV, r0, CONV_ROWS)
                out_ref[r0:r0 + CONV_ROWS, D_INNER + ch:D_INNER + ch + LANES] = _silu(y).astype(BF16)


def _row_specs(tm, d):
    per = tm // HALO
    return (pl.BlockSpec((tm, d), lambda i: (i, 0)),
            pl.BlockSpec((HALO, d), lambda i: (jnp.maximum(i * per - 1, 0), 0)))


def _in_proj(h, seq, norm_w, w_in, conv_w, conv_b, dt_bias):
    t, d = h.shape
    tm = min(2 * ROW_TILE, seq)
    w_main = w_in[:, :D_INNER + CONV_DIM].astype(BF16)
    rep = LANES // SSM_HEADS
    w_dt = jnp.tile(w_in[:, D_INNER + CONV_DIM:], (1, rep)).astype(BF16)
    dtb = jnp.tile(dt_bias, rep).reshape(1, LANES)
    x_spec, xh_spec = _row_specs(tm, d)
    return pl.pallas_call(
        functools.partial(_in_proj_kernel, tm=tm, tiles_per_seq=seq // tm),
        grid=(t // tm,),
        in_specs=[x_spec, xh_spec, _resident((1, d)), _resident(w_main.shape), _resident(w_dt.shape),
                  _resident(conv_w.shape), _resident((1, CONV_DIM)), _resident((1, LANES))],
        out_specs=[pl.BlockSpec((tm, D_INNER + CONV_DIM), lambda i: (i, 0)),
                   pl.BlockSpec((tm, LANES), lambda i: (i, 0))],
        out_shape=[jax.ShapeDtypeStruct((t, D_INNER + CONV_DIM), BF16),
                   jax.ShapeDtypeStruct((t, LANES), F32)],
        scratch_shapes=[pltpu.VMEM((tm + HALO, d), BF16), pltpu.VMEM((CONV_AHEAD + 1, COL_CHUNK // LANES, tm + HALO, LANES), F32)],
        compiler_params=pltpu.CompilerParams(dimension_semantics=("arbitrary",), vmem_limit_bytes=VMEM_LIMIT),
        name="ssm_in_proj",
    )(h, h, norm_w.reshape(1, d), w_main, w_dt, conv_w, conv_b.reshape(1, CONV_DIM), dtb)


def _ssd_kernel(z_ref, xs_ref, b_ref, c_ref, dt_ref, alog_ref, dskip_ref, gnw_ref, ee_ref, ew_ref,
                y_ref, state_ref, yg_ref, e_ref, w_ref):
    @pl.when(pl.program_id(1) == 0)
    def _():
        state_ref[...] = jnp.zeros_like(state_ref)

    n = SSM_CHUNK
    row = lax.broadcasted_iota(jnp.int32, (n, n), 0)
    col = lax.broadcasted_iota(jnp.int32, (n, n), 1)
    causal = col <= row
    low_half = col < SSM_HEAD_DIM
    tri = causal.astype(BF16)

    def hi_lo(x):
        hi = x.astype(BF16)
        return hi, (x - hi.astype(F32)).astype(BF16)

    def decays(u):
        dt = dt_ref[u * n:(u + 1) * n, :]
        ad = dt * (-jnp.exp(alog_ref[...]) * LOG2E)
        a1, a2, a3 = _split3(ad)
        acs = _dot(tri, a1) + _dot(tri, a2) + _dot(tri, a3)
        acs_last = acs[n - 1:n, :]
        src_t = (acs - jnp.log(dt) * LOG2E).T
        e_hi, e_lo = hi_lo(jnp.exp2(acs))
        w_hi, w_lo = hi_lo(jnp.exp2(acs_last - acs) * dt)
        q = SSM_HEADS
        packed = jnp.where(col < q, e_hi, jnp.where(col < 2 * q, e_lo, jnp.where(col < 3 * q, w_hi, w_lo)))
        for c in range(D_INNER // COL_CHUNK):
            lo = c * COL_CHUNK
            e_ref[u, :, lo:lo + COL_CHUNK] = _dot(packed, ee_ref[:, lo:lo + COL_CHUNK])
            w_ref[u, :, lo:lo + COL_CHUNK] = _dot(packed, ew_ref[:, lo:lo + COL_CHUNK])
        return acs, src_t

    terms = [decays(u) for u in range(SSD_SUB)]
    for u in range(SSD_SUB):
        acs, src_t = terms[u]
        rows = slice(u * n, (u + 1) * n)
        for g in range(SSM_GROUPS):
            gl = g * SSM_STATE
            bg = b_ref[rows, gl:gl + SSM_STATE]
            cg = c_ref[rows, gl:gl + SSM_STATE]
            cb = _dot_nt(cg, bg)
            bg_t = bg.astype(F32).T.astype(BF16)
            for p in range(SSM_HEADS // SSM_GROUPS // 2):
                h0 = g * (SSM_HEADS // SSM_GROUPS) + 2 * p
                lo = h0 * SSM_HEAD_DIM
                xs_pair = xs_ref[rows, lo:lo + LANES]
                xs_f = xs_pair.astype(F32)
                ms = []
                for h in (h0, h0 + 1):
                    diff = acs[:, h:h + 1] - src_t[h:h + 1, :]
                    ms.append((cb * jnp.exp2(jnp.where(causal, diff, NEG_BIG))).astype(BF16))
                zero = jnp.zeros_like(xs_pair)
                rhs = jnp.concatenate([jnp.where(low_half, xs_pair, zero), jnp.where(low_half, zero, xs_pair)], axis=0)
                y_diag = _dot(jnp.concatenate(ms, axis=1), rhs)
                s_pair = state_ref[:, lo:lo + LANES]
                y_off = _dot(cg, s_pair.astype(BF16)) * e_ref[u, :, lo:lo + LANES]
                xw = (xs_f * w_ref[u, :, lo:lo + LANES]).astype(BF16)
                state_ref[:, lo:lo + LANES] = s_pair * e_ref[u, n - 1:n, lo:lo + LANES] + _dot(bg_t, xw)
                y = y_diag + y_off + dskip_ref[:, lo:lo + LANES] * xs_f
                yg_ref[u, :, lo:lo + LANES] = y * z_ref[rows, lo:lo + LANES].astype(F32)
            gw = D_INNER // SSM_GROUPS
            yg = yg_ref[u, :, g * gw:(g + 1) * gw]
            y_ref[rows, g * gw:(g + 1) * gw] = _rms(yg, gnw_ref[:, g * gw:(g + 1) * gw]).astype(BF16)


def _head_spread(first_group):
    k = jnp.arange(LANES)[:, None]
    head = jnp.arange(D_INNER)[None, :] // SSM_HEAD_DIM
    group = k // SSM_HEADS
    return ((k % SSM_HEADS == head) & (group >= first_group) & (group < first_group + 2)).astype(BF16)


def _ssd(zx, dt, bsz, seq, a_log, d_skip, gate_norm_w):
    t = zx.shape[0]
    rows = SSD_SUB * SSM_CHUNK
    nc = seq // rows
    alog = jnp.tile(a_log, LANES // SSM_HEADS).reshape(1, LANES)
    dskip = jnp.repeat(d_skip, SSM_HEAD_DIM).reshape(1, D_INNER)
    rowblk = lambda b, c: b * nc + c
    return pl.pallas_call(
        _ssd_kernel,
        grid=(bsz, nc),
        in_specs=[pl.BlockSpec((rows, D_INNER), lambda b, c: (rowblk(b, c), 0)),
                  pl.BlockSpec((rows, D_INNER), lambda b, c: (rowblk(b, c), 1)),
                  pl.BlockSpec((rows, GN), lambda b, c: (rowblk(b, c), 2 * D_INNER // GN)),
                  pl.BlockSpec((rows, GN), lambda b, c: (rowblk(b, c), 2 * D_INNER // GN + 1)),
                  pl.BlockSpec((rows, LANES), lambda b, c: (rowblk(b, c), 0)),
                  _resident((1, LANES)), _resident((1, D_INNER)), _resident((1, D_INNER)),
                  _resident((LANES, D_INNER)), _resident((LANES, D_INNER))],
        out_specs=pl.BlockSpec((rows, D_INNER), lambda b, c: (rowblk(b, c), 0)),
        out_shape=jax.ShapeDtypeStruct((t, D_INNER), BF16),
        scratch_shapes=[pltpu.VMEM((SSM_STATE, D_INNER), F32)] + [pltpu.VMEM((SSD_SUB, SSM_CHUNK, D_INNER), F32)] * 3,
        compiler_params=pltpu.CompilerParams(dimension_semantics=("arbitrary", "arbitrary"),
                                             vmem_limit_bytes=VMEM_LIMIT),
        name="ssd_scan",
    )(zx, zx, zx, zx, dt, alog, dskip, gate_norm_w.reshape(1, D_INNER), _head_spread(0), _head_spread(2))


def _ffn_kernel(a_ref, wp_ref, res_ref, nw_ref, wup_ref, cw_ref, cb_ref, wd_ref, fnw_ref,
                out_ref, xn_ref, tail_ref, h_ref, hid_ref, *, tm, tiles_per_seq, final_norm):
    first = (pl.program_id(0) % tiles_per_seq) == 0
    for c in range(D_MODEL // COL_CHUNK):
        lo = c * COL_CHUNK
        out_ref[:, lo:lo + COL_CHUNK] = res_ref[:, lo:lo + COL_CHUNK] + _dot(a_ref[...], wp_ref[:, lo:lo + COL_CHUNK])

    @pl.when(first)
    def _():
        tail_ref[...] = jnp.zeros_like(tail_ref)

    xn_ref[:HALO, :] = tail_ref[...]
    xn_ref[HALO:, :] = _rms(out_ref[...], nw_ref[...]).astype(BF16)
    tail_ref[...] = xn_ref[tm:, :]
    fc = FFN_CHUNK
    nchunks = D_FF // fc

    nbuf = h_ref.shape[0]

    def up_proj(c):
        buf = h_ref.at[c % nbuf]
        _store_panels(buf, 0, _dot(xn_ref[...], wup_ref[:, c * fc:(c + 1) * fc]))
        _store_panels(buf, fc // LANES, _dot(xn_ref[...], wup_ref[:, D_FF + c * fc:D_FF + (c + 1) * fc]))

    for c in range(min(CONV_AHEAD, nchunks)):
        up_proj(c)
    for c in range(nchunks):
        if c + CONV_AHEAD < nchunks:
            up_proj(c + CONV_AHEAD)
        buf = h_ref.at[c % nbuf]
        for p in range(fc // LANES):
            ch = c * fc + p * LANES
            for r0 in range(0, tm, CONV_ROWS):
                gate = _causal_conv(buf, p, cw_ref, cb_ref, ch, FFN_CONV, r0, CONV_ROWS)
                value = _causal_conv(buf, fc // LANES + p, cw_ref, cb_ref, D_FF + ch, FFN_CONV, r0, CONV_ROWS)
                hid_ref[r0:r0 + CONV_ROWS, ch:ch + LANES] = (_silu(gate) * value).astype(BF16)
    for c in range(D_MODEL // COL_CHUNK):
        lo = c * COL_CHUNK
        out_ref[:, lo:lo + COL_CHUNK] += _dot(hid_ref[...], wd_ref[:, lo:lo + COL_CHUNK])
    if final_norm:
        out_ref[...] = _rms(out_ref[...], fnw_ref[...])


def _proj_ffn(a, w_proj, res, seq, norm_w, w_up, conv_w, conv_b, w_down, final_norm_w=None):
    t, k = a.shape
    d = res.shape[1]
    tm = min(ROW_TILE, seq)
    final_norm = final_norm_w is not None
    fnw = (final_norm_w if final_norm else jnp.ones((d,), F32)).reshape(1, d)
    return pl.pallas_call(
        functools.partial(_ffn_kernel, tm=tm, tiles_per_seq=seq // tm, final_norm=final_norm),
        grid=(t // tm,),
        in_specs=[pl.BlockSpec((tm, k), lambda i: (i, 0)), _resident((k, d)), pl.BlockSpec((tm, d), lambda i: (i, 0)),
                  _resident((1, d)), _resident((d, 2 * D_FF)), _resident((FFN_CONV, 2 * D_FF)),
                  _resident((1, 2 * D_FF)), _resident((D_FF, d)), _resident((1, d))],
        out_specs=pl.BlockSpec((tm, d), lambda i: (i, 0)),
        out_shape=jax.ShapeDtypeStruct((t, d), F32),
        scratch_shapes=[pltpu.VMEM((tm + HALO, d), BF16), pltpu.VMEM((HALO, d), BF16),
                        pltpu.VMEM((CONV_AHEAD + 1, 2 * FFN_CHUNK // LANES, tm + HALO, LANES), F32),
                        pltpu.VMEM((tm, D_FF), BF16)],
        compiler_params=pltpu.CompilerParams(dimension_semantics=("arbitrary",), vmem_limit_bytes=VMEM_LIMIT),
        name="proj_conv_ffn",
    )(a, w_proj.astype(BF16), res, norm_w.reshape(1, d), w_up.astype(BF16), conv_w,
      conv_b.reshape(1, 2 * D_FF), w_down.astype(BF16), fnw)


def _qkv_kernel(x_ref, qnw_ref, kvnw_ref, w_ref, out_ref):
    x = x_ref[...]
    xhat = x * lax.rsqrt(jnp.mean(x * x, axis=-1, keepdims=True) + EPS)
    xq = (xhat * qnw_ref[...]).astype(BF16)
    xkv = (xhat * kvnw_ref[...]).astype(BF16)
    for c in range(3 * D_MODEL // COL_CHUNK):
        lo = c * COL_CHUNK
        if lo < D_MODEL:
            out_ref[:, lo:lo + COL_CHUNK] = (_dot(xq, w_ref[:, lo:lo + COL_CHUNK]) * Q_SCALE).astype(BF16)
        else:
            out_ref[:, lo:lo + COL_CHUNK] = _dot(xkv, w_ref[:, lo:lo + COL_CHUNK]).astype(BF16)


def _qkv_proj(h, seq, q_norm_w, kv_norm_w, w_q, w_k, w_v):
    t, d = h.shape
    tm = min(2 * ROW_TILE, seq)
    w = jnp.concatenate([w_q, w_k, w_v], axis=1).astype(BF16)
    return pl.pallas_call(
        _qkv_kernel,
        grid=(t // tm,),
        in_specs=[pl.BlockSpec((tm, d), lambda i: (i, 0)), _resident((1, d)), _resident((1, d)),
                  _resident((d, 3 * d))],
        out_specs=pl.BlockSpec((tm, 3 * d), lambda i: (i, 0)),
        out_shape=jax.ShapeDtypeStruct((t, 3 * d), BF16),
        compiler_params=pltpu.CompilerParams(dimension_semantics=("arbitrary",), vmem_limit_bytes=VMEM_LIMIT),
        name="qkv_proj",
    )(h, q_norm_w.reshape(1, d), kv_norm_w.reshape(1, d), w)


def _attn_kernel(q_ref, k_ref, v_ref, tt_ref, out_ref, ks_ref, vs_ref):
    i = pl.program_id(2)
    n = SB_BLOCK
    nblk = ks_ref.shape[0] - 1
    row = lax.broadcasted_iota(jnp.int32, (n, n), 0)
    col = lax.broadcasted_iota(jnp.int32, (n, n), 1)
    low_half = col < SB_HEAD_DIM
    before = col < row

    @pl.when(i == 0)
    def _():
        def fill(j, _):
            sl = pl.ds(pl.multiple_of(j * n, n), n)
            kj = k_ref[sl, :]
            vj = v_ref[sl, :]
            zero = jnp.zeros_like(kj)
            ks_ref[j] = jnp.concatenate([jnp.where(low_half, kj, zero), jnp.where(low_half, zero, kj)], axis=0)
            vs_ref[j] = jnp.concatenate([jnp.where(low_half, vj, zero), jnp.where(low_half, zero, vj)], axis=0)
            return 0
        lax.fori_loop(0, nblk, fill, 0)
        ks_ref[nblk] = jnp.zeros(ks_ref.shape[1:], BF16)
        vs_ref[nblk] = jnp.zeros(vs_ref.shape[1:], BF16)

    tt = tt_ref[...]
    tiles = [i * ATTN_TILES + u for u in range(ATTN_TILES)]
    qss = [q_ref[u * n:(u + 1) * n, :] for u in range(ATTN_TILES)]

    def sweep(us, ts, states, diagonal):
        work = [(u, g) for u in range(len(us)) for g in range(ATTN_GROUP)]
        jcs, z2s, sp2s, laters = {}, {}, {}, {}
        for u, g in work:
            j = tiles[us[u]] - (ts[u] * ATTN_GROUP + g)
            jcs[u, g] = jnp.where(j >= 0, j, nblk)
            z2s[u, g] = _dot_nt(qss[us[u]], ks_ref[jcs[u, g]])
        for key in work:
            sp2s[key] = _softplus_log2(z2s[key])
        carries = [list(st[0]) for st in states]
        for u, g in work:
            for h in range(2):
                sp = sp2s[u, g][:, h * n:(h + 1) * n]
                if diagonal and g == 0:
                    sp = jnp.where(before, sp, 0.0)
                cs = _dot(sp.astype(BF16), tt)
                if carries[u][h] is None:
                    laters[u, g, h], carries[u][h] = cs[:, :n], cs[:, n:]
                else:
                    laters[u, g, h] = cs[:, :n] + carries[u][h]
                    carries[u][h] = carries[u][h] + cs[:, n:]
        mins = [jnp.min(jnp.minimum(c[0], c[1])) for c in carries]
        accs = [st[1] for st in states]
        for u, g in work:
            atts = []
            for h in range(2):
                sl = slice(h * n, (h + 1) * n)
                att = jnp.exp2(z2s[u, g][:, sl] - sp2s[u, g][:, sl] - laters[u, g, h])
                if diagonal and g == 0:
                    att = jnp.where(before, att, 0.0)
                atts.append(att.astype(BF16))
            pv = _dot(jnp.concatenate(atts, axis=1), vs_ref[jcs[u, g]])
            accs[u] = pv if accs[u] is None else accs[u] + pv
        return [(carries[u], accs[u], mins[u]) for u in range(len(us))]

    us = list(range(ATTN_TILES))
    first = sweep(us, [0] * ATTN_TILES, [([None, None], None)] * ATTN_TILES, True)
    for u in us:
        carries, acc, min_carry = first[u]
        ngroups = (tiles[u] + ATTN_GROUP) // ATTN_GROUP

        def more(state, ngroups=ngroups):
            return jnp.logical_and(state[0] < ngroups, state[1] <= ATTN_STOP)

        def group(state, u=u):
            t, _, c0, c1, acc = state
            (carries, acc, min_carry), = sweep([u], [t], [([c0, c1], acc)], False)
            return t + 1, min_carry, carries[0], carries[1], acc

        state = lax.while_loop(more, group, (jnp.int32(1), min_carry, carries[0], carries[1], acc))
        out_ref[u * n:(u + 1) * n, :] = state[4].astype(BF16)


def _attention(qkv, bsz, seq):
    t = qkv.shape[0]
    n = SB_BLOCK
    nq = seq // n
    pairs = SB_HEADS // 2
    r = jnp.arange(n)
    tri = (r[:, None] > r[None, :]).astype(BF16)
    tt = jnp.concatenate([tri, jnp.ones((n, n), BF16)], axis=1)
    return pl.pallas_call(
        _attn_kernel,
        grid=(bsz, pairs, nq // ATTN_TILES),
        in_specs=[pl.BlockSpec((ATTN_TILES * n, LANES), lambda b, p, i: (b * (nq // ATTN_TILES) + i, p)),
                  pl.BlockSpec((seq, LANES), lambda b, p, i: (b, pairs + p)),
                  pl.BlockSpec((seq, LANES), lambda b, p, i: (b, 2 * pairs + p)),
                  _resident((n, 2 * n))],
        out_specs=pl.BlockSpec((ATTN_TILES * n, LANES), lambda b, p, i: (b * (nq // ATTN_TILES) + i, p)),
        out_shape=jax.ShapeDtypeStruct((t, SB_HEADS * SB_HEAD_DIM), BF16),
        scratch_shapes=[pltpu.VMEM((nq + 1, 2 * n, LANES), BF16), pltpu.VMEM((nq + 1, 2 * n, LANES), BF16)],
        compiler_params=pltpu.CompilerParams(dimension_semantics=("arbitrary", "arbitrary", "arbitrary"),
                                             vmem_limit_bytes=VMEM_LIMIT),
        name="stick_breaking_attention",
    )(qkv, qkv, qkv, tt)


def kernel(x, ssm_norm_w, ssm_in_w, ssm_conv_w, ssm_conv_b, ssm_dt_bias, ssm_a_log, ssm_d, ssm_gate_norm_w, ssm_out_w, kv_norm_w, w_k, w_v, attn_norm_w, w_q, w_o, ffn_norm_w, ffn_up_w, ffn_conv_w, ffn_conv_b, ffn_down_w, final_norm_w):
    bsz, seq, d = x.shape
    h = x.reshape(bsz * seq, d)
    zx, dt = _in_proj(h, seq, ssm_norm_w[0], ssm_in_w[0], ssm_conv_w[0], ssm_conv_b[0], ssm_dt_bias[0])
    y = _ssd(zx, dt, bsz, seq, ssm_a_log[0], ssm_d[0], ssm_gate_norm_w[0])
    h = _proj_ffn(y, ssm_out_w[0], h, seq, ffn_norm_w[0], ffn_up_w[0], ffn_conv_w[0], ffn_conv_b[0], ffn_down_w[0])
    qkv = _qkv_proj(h, seq, attn_norm_w[0], kv_norm_w, w_q[0], w_k, w_v)
    o = _attention(qkv, bsz, seq)
    h = _proj_ffn(o, w_o[0], h, seq, ffn_norm_w[1], ffn_up_w[1], ffn_conv_w[1], ffn_conv_b[1], ffn_down_w[1],
                  final_norm_w)
    return h.reshape(bsz, seq, d)
```
